```python
import math
import jax, jax.numpy as jnp
from jax import lax
import numpy as np

D_MODEL = 2048
BATCH = 1
SEQ = 8192
DEPTH = 2
DEC_BATCH = 128
DEC_SEQ = 4
PAST_LEN = 2048
PAGE_SIZE = 128

BRANCH_W = D_MODEL // 2
N_BRANCH = 3
POOL_W = BRANCH_W
POOL_WINDOWS = (2, 4, 8, 16)
POOL_G = POOL_W // len(POOL_WINDOWS)
POOL_BUF = max(POOL_WINDOWS) - 1
HEAD_DIM = 128
H_DIFF = BRANCH_W // HEAD_DIM
DK_DIFF = HEAD_DIM // 2
H_SB = BRANCH_W // HEAD_DIM
H_KV = H_DIFF + H_SB
ROPE_THETA = 10000.0
Q_BLOCK = 128
EPS = 1e-6
SUBLN_EPS = 1e-5
IN_SIZES = (POOL_W, POOL_W, H_DIFF * 2 * DK_DIFF, H_SB * HEAD_DIM, H_KV * HEAD_DIM, H_KV * HEAD_DIM,
            BRANCH_W, BRANCH_W, N_BRANCH * D_MODEL)
IN_COLS = sum(IN_SIZES)
IN_OFFSETS = tuple(int(o) for o in np.cumsum(IN_SIZES)[:-1])

kernel_name = "hybrid_pool_diffattn_stickbreak_decoder_step"


def lambda_init(layer):
    return 0.8 - 0.6 * math.exp(-0.3 * layer)


def rmsnorm(x, w, eps=EPS):
    xf = x.astype(jnp.float32)
    y = xf * lax.rsqrt(jnp.mean(xf * xf, axis=-1, keepdims=True) + eps)
    return (y * w.astype(jnp.float32)).astype(x.dtype)


def rope(x, pos):
    half = x.shape[-1] // 2
    inv = 1.0 / (ROPE_THETA ** (jnp.arange(half, dtype=jnp.float32) * 2.0 / x.shape[-1]))
    ang = pos.astype(jnp.float32)[:, None] * inv[None, :]
    cos = jnp.cos(ang)[None, :, None, None, :]
    sin = jnp.sin(ang)[None, :, None, None, :]
    xf = x.astype(jnp.float32)
    x1, x2 = xf[..., :half], xf[..., half:]
    return jnp.concatenate([x1 * cos - x2 * sin, x2 * cos + x1 * sin], axis=-1).astype(x.dtype)


def pool_mix(u_all, n_new, pool_w, pool_scale):
    B, L, P = u_all.shape
    uf = u_all.astype(jnp.float32)
    cs = jnp.concatenate([jnp.zeros((B, 1, P), jnp.float32), jnp.cumsum(uf, axis=1)], axis=1)
    r = np.arange(L - n_new, L)
    hi = cs[:, L - n_new + 1:]
    u_new = uf[:, L - n_new:]
    outs = []
    for g, w in enumerate(POOL_WINDOWS):
        lo = np.maximum(r + 1 - w, 0)
        cnt = jnp.asarray(np.minimum(w, r + 1).astype(np.float32))[None, :, None]
        sl = slice(g * POOL_G, (g + 1) * POOL_G)
        mean = (hi[..., sl] - cs[:, lo, sl]) / cnt
        outs.append(mean - u_new[..., sl])
    d = jnp.stack(outs, axis=2)
    y = jnp.einsum('btgc,gcd->btgd', d, pool_w.astype(jnp.float32)).reshape(B, n_new, P)
    return y * pool_scale.astype(jnp.float32)


def mixer_attention(q_d, q_s, segs, q_pos, lam):
    sd_l, ss_l, mc_l, ms_l, lens = [], [], [], [], []
    for k, v, k_pos in segs:
        B, Tk = k.shape[:2]
        kd = k[:, :, :H_DIFF].reshape(B, Tk, H_DIFF, 2, DK_DIFF)
        sd_l.append(jnp.einsum('bqhcd,bkhcd->bchqk', q_d, kd).astype(jnp.float32))
        ss_l.append(jnp.einsum('bqhd,bkhd->bhqk', q_s, k[:, :, H_DIFF:]).astype(jnp.float32))
        mc_l.append(k_pos[None, :] <= q_pos[:, None])
        ms_l.append(k_pos[None, :] < q_pos[:, None])
        lens.append(Tk)
    sd = jnp.concatenate(sd_l, axis=-1) * (DK_DIFF ** -0.5)
    ss = jnp.concatenate(ss_l, axis=-1) * (HEAD_DIM ** -0.5)
    mask_c = jnp.concatenate(mc_l, axis=-1)
    mask_s = jnp.concatenate(ms_l, axis=-1)
    p = jax.nn.softmax(jnp.where(mask_c, sd, -jnp.inf), axis=-1)
    w_d = p[:, 0] - lam * p[:, 1]
    log_beta = jax.nn.log_sigmoid(ss)
    log_rest = jnp.where(mask_s, jax.nn.log_sigmoid(-ss), 0.0)
    suffix = lax.cumsum(log_rest, axis=ss.ndim - 1, reverse=True) - log_rest
    a_s = jnp.where(mask_s, jnp.exp(log_beta + suffix), 0.0)
    o_d, o_s, off = 0.0, 0.0, 0
    for (k, v, _), tk in zip(segs, lens):
        o_d = o_d + jnp.einsum('bhqk,bkhe->bqhe', w_d[..., off:off + tk], v[:, :, :H_DIFF].astype(jnp.float32))
        o_s = o_s + jnp.einsum('bhqk,bkhe->bqhe', a_s[..., off:off + tk], v[:, :, H_DIFF:].astype(jnp.float32))
        off += tk
    return o_d, o_s


def trunk_layer(x, layer, pos, pool_ctx, past, norm_w, w_in, gate_b, pool_w, pool_scale,
                diff_lambda, diff_subln_w, w_branch, w_out):
    B, T, _ = x.shape
    h = rmsnorm(x, norm_w)
    z = h @ w_in
    u, g_pool, q_d, q_s, k, v, g_diff, g_sb, g_br = jnp.split(z, IN_OFFSETS, axis=-1)
    u_all = jnp.concatenate([pool_ctx.astype(u.dtype), u], axis=1)
    y_pool = pool_mix(u_all, T, pool_w, pool_scale)
    new_pool = u_all[:, -POOL_BUF:]
    q_d = rope(q_d.reshape(B, T, H_DIFF, 2, DK_DIFF), pos)
    q_s = q_s.reshape(B, T, H_SB, HEAD_DIM)
    k = k.reshape(B, T, H_KV, HEAD_DIM)
    k_d = rope(k[:, :, :H_DIFF].reshape(B, T, H_DIFF, 2, DK_DIFF), pos).reshape(B, T, H_DIFF, HEAD_DIM)
    k = jnp.concatenate([k_d, k[:, :, H_DIFF:]], axis=2)
    v = v.reshape(B, T, H_KV, HEAD_DIM)
    lam0 = lambda_init(layer)
    lf = diff_lambda.astype(jnp.float32)
    lam = jnp.exp(jnp.sum(lf[0] * lf[1])) - jnp.exp(jnp.sum(lf[2] * lf[3])) + lam0
    if past is None:
        nb = T // Q_BLOCK
        qd_b = jnp.moveaxis(q_d.reshape(B, nb, Q_BLOCK, H_DIFF, 2, DK_DIFF), 1, 0)
        qs_b = jnp.moveaxis(q_s.reshape(B, nb, Q_BLOCK, H_SB, HEAD_DIM), 1, 0)
        pos_b = pos.reshape(nb, Q_BLOCK)

        def block(args):
            qd_i, qs_i, pos_i = args
            return mixer_attention(qd_i, qs_i, [(k, v, pos)], pos_i, lam)

        o_d, o_s = lax.map(block, (qd_b, qs_b, pos_b))
        o_d = jnp.moveaxis(o_d, 0, 1).reshape(B, T, H_DIFF, HEAD_DIM)
        o_s = jnp.moveaxis(o_s, 0, 1).reshape(B, T, H_SB, HEAD_DIM)
    else:
        k_past, v_past, past_pos = past
        o_d, o_s = mixer_attention(q_d, q_s, [(k_past, v_past, past_pos), (k, v, pos)], pos, lam)
    o_d = rmsnorm(o_d, diff_subln_w, SUBLN_EPS) * (1.0 - lam0)
    y_diff = o_d.reshape(B, T, BRANCH_W)
    y_sb = o_s.reshape(B, T, BRANCH_W)
    branches = jnp.stack([y_pool * jax.nn.silu(g_pool), y_diff * jax.nn.silu(g_diff),
                          y_sb * jax.nn.silu(g_sb)], axis=2).astype(x.dtype)
    proj = jnp.einsum('btnc,ncd->btnd', branches, w_branch)
    gates = jax.nn.sigmoid(g_br.reshape(B, T, N_BRANCH, D_MODEL) + gate_b)
    merged = jnp.sum(gates * proj, axis=2)
    x = x + (merged @ w_out).astype(x.dtype)
    return x, k, v, new_pool


def setup_inputs(seed: int = 0) -> dict:
    key = jax.random.key(seed)
    ks = jax.random.split(key, 18)
    f32 = jnp.float32
    n_pages = PAST_LEN // PAGE_SIZE
    in_use = DEC_BATCH * n_pages
    n_phys = in_use + max(1, in_use // 4)
    x_prompt = jax.random.normal(ks[0], (BATCH, SEQ, D_MODEL), f32)
    x_sample = jax.random.normal(ks[1], (DEC_BATCH, DEC_SEQ, D_MODEL), f32)
    cache_k = jax.random.normal(ks[2], (DEPTH, n_phys, PAGE_SIZE, H_KV, HEAD_DIM), f32)
    cache_v = jax.random.normal(ks[3], (DEPTH, n_phys, PAGE_SIZE, H_KV, HEAD_DIM), f32)
    state_pool = jax.random.normal(ks[4], (DEPTH, DEC_BATCH, POOL_BUF, POOL_W), f32)
    page_table = jax.random.permutation(ks[5], n_phys)[:in_use].reshape(DEC_BATCH, n_pages).astype(jnp.int32)
    norm_w = 1.0 + 0.1 * jax.random.normal(ks[6], (DEPTH, D_MODEL), f32)
    w_in = jax.random.normal(ks[7], (DEPTH, D_MODEL, IN_COLS), f32) * D_MODEL ** -0.5
    gate_b = 0.1 * jax.random.normal(ks[8], (DEPTH, N_BRANCH, D_MODEL), f32)
    pool_w = jax.random.normal(ks[9], (DEPTH, len(POOL_WINDOWS), POOL_G, POOL_G), f32) * POOL_G ** -0.5
    pool_scale = 1.0 + 0.1 * jax.random.normal(ks[10], (DEPTH, POOL_W), f32)
    diff_lambda = 0.1 * jax.random.normal(ks[11], (DEPTH, 4, DK_DIFF), f32)
    diff_subln_w = 1.0 + 0.1 * jax.random.normal(ks[12], (DEPTH, HEAD_DIM), f32)
    w_branch = jax.random.normal(ks[13], (DEPTH, N_BRANCH, BRANCH_W, D_MODEL), f32) * BRANCH_W ** -0.5
    w_out = jax.random.normal(ks[14], (DEPTH, D_MODEL, D_MODEL), f32) * D_MODEL ** -0.5
    norm_f_w = 1.0 + 0.1 * jax.random.normal(ks[15], (D_MODEL,), f32)
    return {"x_prompt": x_prompt, "x_sample": x_sample, "cache_k": cache_k, "cache_v": cache_v,
            "state_pool": state_pool, "page_table": page_table, "norm_w": norm_w, "w_in": w_in,
            "gate_b": gate_b, "pool_w": pool_w, "pool_scale": pool_scale, "diff_lambda": diff_lambda,
            "diff_subln_w": diff_subln_w, "w_branch": w_branch, "w_out": w_out, "norm_f_w": norm_f_w}


def reference(x_prompt, x_sample, cache_k, cache_v, state_pool, page_table, norm_w, w_in, gate_b,
              pool_w, pool_scale, diff_lambda, diff_subln_w, w_branch, w_out, norm_f_w):
    bp, sp = x_prompt.shape[0], x_prompt.shape[1]
    bs, ts = x_sample.shape[0], x_sample.shape[1]
    past_len = page_table.shape[1] * PAGE_SIZE
    pos_p = jnp.arange(sp, dtype=jnp.int32)
    pos_s = past_len + jnp.arange(ts, dtype=jnp.int32)
    past_pos = jnp.arange(past_len, dtype=jnp.int32)
    xp, xs = x_prompt, x_sample
    kp_l, vp_l, pp_l, ks_l, vs_l, ps_l = [], [], [], [], [], []
    for l in range(DEPTH):
        lw = (norm_w[l], w_in[l], gate_b[l], pool_w[l], pool_scale[l], diff_lambda[l],
              diff_subln_w[l], w_branch[l], w_out[l])
        xp, kp, vp, pp = trunk_layer(xp, l, pos_p, jnp.zeros((bp, 0, POOL_W), xp.dtype), None, *lw)
        k_past = cache_k[l][page_table].reshape(bs, past_len, H_KV, HEAD_DIM)
        v_past = cache_v[l][page_table].reshape(bs, past_len, H_KV, HEAD_DIM)
        xs, kn, vn, pn = trunk_layer(xs, l, pos_s, state_pool[l], (k_past, v_past, past_pos), *lw)
        kp_l.append(kp); vp_l.append(vp); pp_l.append(pp)
        ks_l.append(kn); vs_l.append(vn); ps_l.append(pn)
    y_prompt = rmsnorm(xp, norm_f_w)
    y_sample = rmsnorm(xs, norm_f_w)
    return (y_prompt, y_sample, jnp.stack(kp_l), jnp.stack(vp_l), jnp.stack(pp_l),
            jnp.stack(ks_l), jnp.stack(vs_l), jnp.stack(ps_l))
```

```python
import functools
import math

import jax
import jax.numpy as jnp
from jax import lax
from jax.experimental import pallas as pl
from jax.experimental.pallas import tpu as pltpu

POOL_WINDOWS = (2, 4, 8, 16)
POOL_BUF = max(POOL_WINDOWS) - 1
POOL_HALO = 16
HEAD_DIM = 128
DK_DIFF = HEAD_DIM // 2
ROPE_THETA = 10000.0
EPS = 1e-6
SUBLN_EPS = 1e-5
NEG_BIG = -1e30
SB_CUTOFF = -104.0
HEAD_ROWS = 16
VMEM_LIMIT_BYTES = 56 * 1024 * 1024

_BF16 = jnp.bfloat16
_F32 = jnp.float32


def _lambda_init(layer):
    return 0.8 - 0.6 * math.exp(-0.3 * layer)


def _params(*sem):
    return pltpu.CompilerParams(dimension_semantics=sem, vmem_limit_bytes=VMEM_LIMIT_BYTES)


def _pow2_tile(cap, *sizes):
    t = cap
    while any(s % t for s in sizes):
        t //= 2
    return t


def _rmsnorm_kernel(x_ref, w_ref, o_ref, *, eps):
    x = x_ref[...]
    ms = jnp.mean(x * x, axis=-1, keepdims=True)
    o_ref[...] = (x * lax.rsqrt(ms + eps) * w_ref[...]).astype(o_ref.dtype)


def _rmsnorm(x, w, out_dtype, tm):
    m, d = x.shape
    return pl.pallas_call(
        functools.partial(_rmsnorm_kernel, eps=EPS),
        grid=(m // tm,),
        in_specs=[pl.BlockSpec((tm, d), lambda i: (i, 0)), pl.BlockSpec((1, d), lambda i: (0, 0))],
        out_specs=pl.BlockSpec((tm, d), lambda i: (i, 0)),
        out_shape=jax.ShapeDtypeStruct((m, d), out_dtype),
        compiler_params=_params("parallel"),
        name="rmsnorm",
    )(x, w.reshape(1, d))


def _rope_heads(z, cos, sin_signed):
    tm = z.shape[0]
    lane = lax.broadcasted_iota(jnp.int32, (tm, HEAD_DIM), 1)
    first_half = (lane % DK_DIFF) < (DK_DIFF // 2)
    outs = []
    for h in range(z.shape[1] // HEAD_DIM):
        zh = z[:, h * HEAD_DIM:(h + 1) * HEAD_DIM]
        partner = jnp.where(first_half, pltpu.roll(zh, HEAD_DIM - DK_DIFF // 2, 1), pltpu.roll(zh, DK_DIFF // 2, 1))
        outs.append(zh * cos + partner * sin_signed)
    return jnp.concatenate(outs, axis=1)


def _proj_kernel(*refs, epilogue, scale):
    h_ref, w_ref = refs[0], refs[1]
    z = jnp.dot(h_ref[...], w_ref[...], preferred_element_type=_F32)
    if epilogue == "plain":
        refs[2][...] = z
    elif epilogue == "silu":
        refs[2][...] = z * jax.nn.sigmoid(z)
    elif epilogue == "scale_bf16":
        refs[2][...] = (z * scale).astype(_BF16)
    elif epilogue == "rope_scale_bf16":
        cos_ref, sin_ref, o_ref = refs[2:]
        o_ref[...] = (_rope_heads(z, cos_ref[...], sin_ref[...]) * scale).astype(_BF16)
    elif epilogue == "key":
        cos_ref, sin_ref, o32_ref, o16_ref = refs[2:]
        j = pl.program_id(0)

        @pl.when(j == 0)
        def _():
            r = _rope_heads(z, cos_ref[...], sin_ref[...])
            o32_ref[...] = r
            o16_ref[...] = r.astype(_BF16)

        @pl.when(j != 0)
        def _():
            o32_ref[...] = z
            o16_ref[...] = z.astype(_BF16)
    elif epilogue == "value":
        o32_ref, o16_ref = refs[2:]
        o32_ref[...] = z
        o16_ref[...] = z.astype(_BF16)
    elif epilogue == "gate":
        b_ref, o_ref = refs[2:]
        o_ref[...] = jax.nn.sigmoid(z + b_ref[...])
    else:
        raise ValueError(epilogue)


def _proj(h, w, col_off, n_cols, epilogue, *, tm, tn, extras=(), scale=1.0):
    m, d = h.shape
    assert col_off % tn == 0 and n_cols % tn == 0
    off_blk = col_off // tn
    grid = (n_cols // tn, m // tm)
    in_specs = [pl.BlockSpec((tm, d), lambda j, i: (i, 0)),
                pl.BlockSpec((d, tn), lambda j, i: (0, off_blk + j))]
    row_tab = pl.BlockSpec((tm, HEAD_DIM), lambda j, i: (i, 0))
    out_tile = pl.BlockSpec((tm, tn), lambda j, i: (i, j))
    f32_out = jax.ShapeDtypeStruct((m, n_cols), _F32)
    b16_out = jax.ShapeDtypeStruct((m, n_cols), _BF16)
    if epilogue in ("plain", "silu"):
        out_specs, out_shape = out_tile, f32_out
    elif epilogue == "scale_bf16":
        out_specs, out_shape = out_tile, b16_out
    elif epilogue == "rope_scale_bf16":
        in_specs += [row_tab, row_tab]
        out_specs, out_shape = out_tile, b16_out
    elif epilogue == "key":
        assert n_cols == 2 * tn
        in_specs += [row_tab, row_tab]
        out_specs, out_shape = [out_tile, out_tile], [f32_out, b16_out]
    elif epilogue == "value":
        out_specs, out_shape = [out_tile, out_tile], [f32_out, b16_out]
    elif epilogue == "gate":
        in_specs += [pl.BlockSpec((1, tn), lambda j, i: (0, j))]
        out_specs, out_shape = out_tile, f32_out
    else:
        raise ValueError(epilogue)
    return pl.pallas_call(
        functools.partial(_proj_kernel, epilogue=epilogue, scale=scale),
        grid=grid, in_specs=in_specs, out_specs=out_specs, out_shape=out_shape,
        compiler_params=_params("parallel", "parallel"),
        name="proj_" + epilogue,
    )(h, w, *extras)


def _pool_prompt_kernel(u_ref, prev_ref, d_ref, *, tm, pg):
    i = pl.program_id(0)
    u = u_ref[...]
    prev = jnp.where(i > 0, prev_ref[...], 0.0)
    ext = jnp.concatenate([prev, u], axis=0)
    t = i * tm + lax.broadcasted_iota(jnp.int32, (tm, 1), 0)
    for g, w in enumerate(POOL_WINDOWS):
        s = ext[:, g * pg:(g + 1) * pg]
        sh = 1
        while sh < w:
            s = s + pltpu.roll(s, sh, 0)
            sh *= 2
        cnt = jnp.minimum(w, t + 1).astype(_F32)
        d_ref[:, g * pg:(g + 1) * pg] = s[POOL_HALO:] / cnt - u[:, g * pg:(g + 1) * pg]


def _pool_prompt(u, tp, tm):
    m, p = u.shape
    pg = p // len(POOL_WINDOWS)
    hb = tm // POOL_HALO
    return pl.pallas_call(
        functools.partial(_pool_prompt_kernel, tm=tm, pg=pg),
        grid=(tp // tm,),
        in_specs=[pl.BlockSpec((tm, p), lambda i: (i, 0)),
                  pl.BlockSpec((POOL_HALO, p), lambda i: (jnp.maximum(i * hb - 1, 0), 0))],
        out_specs=pl.BlockSpec((tm, p), lambda i: (i, 0)),
        out_shape=jax.ShapeDtypeStruct((m, p), _F32),
        compiler_params=_params("parallel"),
        name="pool_prompt",
    )(u, u)


def _pool_sample_kernel(st_ref, un_ref, d_ref, *, pg):
    nb, ts = st_ref.shape[0], un_ref.shape[0]
    for g, w in enumerate(POOL_WINDOWS):
        cols = slice(g * pg, (g + 1) * pg)
        rows = [st_ref[r, :, cols] for r in range(nb)] + [un_ref[t, :, cols] for t in range(ts)]
        for t in range(ts):
            r = nb + t
            lo = max(r + 1 - w, 0)
            s = rows[lo]
            for rr in range(lo + 1, r + 1):
                s = s + rows[rr]
            d_ref[t, :, cols] = s / float(r + 1 - lo) - rows[r]


def _pool_sample(state_t, u_new_t):
    nb, bs, p = state_t.shape
    ts = u_new_t.shape[0]
    pg = p // len(POOL_WINDOWS)
    return pl.pallas_call(
        functools.partial(_pool_sample_kernel, pg=pg),
        grid=(1,),
        in_specs=[pl.BlockSpec((nb, bs, p), lambda i: (0, 0, 0)), pl.BlockSpec((ts, bs, p), lambda i: (0, 0, 0))],
        out_specs=pl.BlockSpec((ts, bs, p), lambda i: (0, 0, 0)),
        out_shape=jax.ShapeDtypeStruct((ts, bs, p), _F32),
        compiler_params=_params("arbitrary"),
        name="pool_sample",
    )(state_t, u_new_t)


def _lambda_full(lam_ref, lam0):
    dl = lam_ref[...]
    a = jnp.sum(dl[0:1] * dl[1:2], axis=-1, keepdims=True)
    b = jnp.sum(dl[2:3] * dl[3:4], axis=-1, keepdims=True)
    return jnp.exp(a) - jnp.exp(b) + lam0


def _dot_nt(a, b):
    return lax.dot_general(a, b, (((1,), (1,)), ((), ())), preferred_element_type=_F32)


def _diff_prompt_kernel(lam_ref, subw_ref, q_ref, k_ref, v_ref, o_ref, m_scr, l_scr, acc_scr, *, lam0, tq):
    i = pl.program_id(1)
    q = q_ref[...]
    lane = lax.broadcasted_iota(jnp.int32, q.shape, 1)
    zero = jnp.zeros_like(q)
    qq = jnp.concatenate([jnp.where(lane < DK_DIFF, q, zero), jnp.where(lane >= DK_DIFF, q, zero)], axis=0)
    m_scr[...] = jnp.full(m_scr.shape, NEG_BIG, _F32)
    l_scr[...] = jnp.zeros(l_scr.shape, _F32)
    acc_scr[...] = jnp.zeros(acc_scr.shape, _F32)

    def block(j, masked):
        start = pl.multiple_of(j * tq, tq)
        kj = k_ref[pl.ds(start, tq), :]
        vj = v_ref[pl.ds(start, tq), :]
        s = _dot_nt(qq, kj)
        if masked:
            row = lax.broadcasted_iota(jnp.int32, s.shape, 0) % tq
            col = lax.broadcasted_iota(jnp.int32, s.shape, 1)
            s = jnp.where(col <= row, s, NEG_BIG)
        m_old = m_scr[...]
        m_new = jnp.maximum(m_old, jnp.max(s, axis=-1, keepdims=True))
        alpha = jnp.exp(m_old - m_new)
        p = jnp.exp(s - m_new)
        l_scr[...] = alpha * l_scr[...] + jnp.sum(p, axis=-1, keepdims=True)
        acc_scr[...] = alpha * acc_scr[...] + jnp.dot(p.astype(_BF16), vj, preferred_element_type=_F32)
        m_scr[...] = m_new

    def body(j, carry):
        block(j, False)
        return carry

    lax.fori_loop(0, i, body, 0)
    block(i, True)

    o = acc_scr[...] / l_scr[...]
    lam = _lambda_full(lam_ref, lam0)
    od = o[:tq] - lam * o[tq:]
    ms = jnp.mean(od * od, axis=-1, keepdims=True)
    o_ref[...] = od * lax.rsqrt(ms + SUBLN_EPS) * subw_ref[...] * (1.0 - lam0)


def _diff_prompt(lam_p, subw, q_d, k16, v16, tp, n_heads, m_rows, lam0, tq):
    return pl.pallas_call(
        functools.partial(_diff_prompt_kernel, lam0=lam0, tq=tq),
        grid=(n_heads, tp // tq),
        in_specs=[pl.BlockSpec(lam_p.shape, lambda h, i: (0, 0)),
                  pl.BlockSpec((1, HEAD_DIM), lambda h, i: (0, 0)),
                  pl.BlockSpec((tq, HEAD_DIM), lambda h, i: (i, h)),
                  pl.BlockSpec((tp, HEAD_DIM), lambda h, i: (0, h)),
                  pl.BlockSpec((tp, HEAD_DIM), lambda h, i: (0, h))],
        out_specs=pl.BlockSpec((tq, HEAD_DIM), lambda h, i: (i, h)),
        out_shape=jax.ShapeDtypeStruct((m_rows, n_heads * HEAD_DIM), _F32),
        scratch_shapes=[pltpu.VMEM((2 * tq, 1), _F32), pltpu.VMEM((2 * tq, 1), _F32),
                        pltpu.VMEM((2 * tq, HEAD_DIM), _F32)],
        compiler_params=_params("parallel", "arbitrary"),
        name="diff_prompt",
    )(lam_p, subw, q_d, k16, v16)


def _log_sigmoid_pair(s):
    t = jnp.log1p(jnp.exp(-jnp.abs(s)))
    return jnp.minimum(s, 0.0) - t, -jnp.maximum(s, 0.0) - t


def _suffix_sums(lr, tri):
    hi = lr.astype(_BF16)
    lo = (lr - hi.astype(_F32)).astype(_BF16)
    return (jnp.dot(hi, tri, preferred_element_type=_F32) + jnp.dot(lo, tri, preferred_element_type=_F32))


def _strict_lower_ones(n):
    r = lax.broadcasted_iota(jnp.int32, (n, n), 0)
    c = lax.broadcasted_iota(jnp.int32, (n, n), 1)
    return (r > c).astype(_BF16)


def _sb_prompt_kernel(q_ref, k_ref, v_ref, o_ref, c_scr, acc_scr, *, tq):
    i = pl.program_id(1)
    q = q_ref[...]
    tri = _strict_lower_ones(tq)
    c_scr[...] = jnp.zeros(c_scr.shape, _F32)
    acc_scr[...] = jnp.zeros(acc_scr.shape, _F32)

    def block(j, masked):
        start = pl.multiple_of(j * tq, tq)
        kj = k_ref[pl.ds(start, tq), :]
        vj = v_ref[pl.ds(start, tq), :]
        s = _dot_nt(q, kj)
        lb, lr = _log_sigmoid_pair(s)
        if masked:
            row = lax.broadcasted_iota(jnp.int32, s.shape, 0)
            col = lax.broadcasted_iota(jnp.int32, s.shape, 1)
            valid = col < row
            lr = jnp.where(valid, lr, 0.0)
        c = c_scr[...]
        a = jnp.exp(lb + _suffix_sums(lr, tri) + c)
        if masked:
            a = jnp.where(valid, a, 0.0)
        acc_scr[...] += jnp.dot(a.astype(_BF16), vj, preferred_element_type=_F32)
        c_scr[...] = c + jnp.sum(lr, axis=-1, keepdims=True)

    block(i, True)

    def cond(j):
        return jnp.logical_and(j >= 0, jnp.max(c_scr[...]) > SB_CUTOFF)

    def body(j):
        block(j, False)
        return j - 1

    lax.while_loop(cond, body, i - 1)
    o_ref[...] = acc_scr[...]


def _sb_prompt(q_s, k16, v16, tp, n_heads, head_off, m_rows, tq):
    return pl.pallas_call(
        functools.partial(_sb_prompt_kernel, tq=tq),
        grid=(n_heads, tp // tq),
        in_specs=[pl.BlockSpec((tq, HEAD_DIM), lambda h, i: (i, h)),
                  pl.BlockSpec((tp, HEAD_DIM), lambda h, i: (0, head_off + h)),
                  pl.BlockSpec((tp, HEAD_DIM), lambda h, i: (0, head_off + h))],
        out_specs=pl.BlockSpec((tq, HEAD_DIM), lambda h, i: (i, h)),
        out_shape=jax.ShapeDtypeStruct((m_rows, n_heads * HEAD_DIM), _F32),
        scratch_shapes=[pltpu.VMEM((tq, 1), _F32), pltpu.VMEM((tq, HEAD_DIM), _F32)],
        compiler_params=_params("parallel", "arbitrary"),
        name="sb_prompt",
    )(q_s, k16, v16)


def _sample_attn_kernel(pt_ref, lam_ref, subw_ref, q_ref, kn_ref, vn_ref, kc_ref, vc_ref, od_ref, os_ref,
                        knp_scr, vnp_scr, m_scr, l_scr, accd_scr, c_scr, accs_scr,
                        *, lam0, n_diff, n_sb, ts, page):
    del pt_ref
    p = pl.program_id(1)
    n_heads = n_diff + n_sb
    rd = n_diff * HEAD_ROWS
    tri = _strict_lower_ones(page)
    q_all = q_ref[...]

    def process(get_k, get_v, new_block):
        s_parts = []
        for h in range(n_heads):
            s_parts.append(_dot_nt(q_all[h * HEAD_ROWS:(h + 1) * HEAD_ROWS], get_k(h)))
        s_d = jnp.concatenate(s_parts[:n_diff], axis=0)
        s_s = jnp.concatenate(s_parts[n_diff:], axis=0)
        lb, lr = _log_sigmoid_pair(s_s)
        if new_block:
            tok_d = lax.broadcasted_iota(jnp.int32, s_d.shape, 1)
            qi_d = (lax.broadcasted_iota(jnp.int32, s_d.shape, 0) % HEAD_ROWS) % ts
            s_d = jnp.where(jnp.logical_and(tok_d < ts, tok_d <= qi_d), s_d, NEG_BIG)
            tok_s = lax.broadcasted_iota(jnp.int32, s_s.shape, 1)
            qi_s = (lax.broadcasted_iota(jnp.int32, s_s.shape, 0) % HEAD_ROWS) % ts
            valid_s = jnp.logical_and(tok_s < ts, tok_s < qi_s)
            lr = jnp.where(valid_s, lr, 0.0)
        m_old = m_scr[...]
        m_new = jnp.maximum(m_old, jnp.max(s_d, axis=-1, keepdims=True))
        alpha = jnp.exp(m_old - m_new)
        pd = jnp.exp(s_d - m_new)
        l_scr[...] = alpha * l_scr[...] + jnp.sum(pd, axis=-1, keepdims=True)
        m_scr[...] = m_new
        c = c_scr[...]
        a = jnp.exp(lb + _suffix_sums(lr, tri) + c)
        if new_block:
            a = jnp.where(valid_s, a, 0.0)
        c_scr[...] = c + jnp.sum(lr, axis=-1, keepdims=True)
        pd16 = pd.astype(_BF16)
        a16 = a.astype(_BF16)
        pv_d, pv_s = [], []
        for h in range(n_diff):
            pv_d.append(jnp.dot(pd16[h * HEAD_ROWS:(h + 1) * HEAD_ROWS], get_v(h), preferred_element_type=_F32))
        for h in range(n_sb):
            pv_s.append(jnp.dot(a16[h * HEAD_ROWS:(h + 1) * HEAD_ROWS], get_v(n_diff + h),
                                preferred_element_type=_F32))
        accd_scr[...] = alpha * accd_scr[...] + jnp.concatenate(pv_d, axis=0)
        accs_scr[...] += jnp.concatenate(pv_s, axis=0)

    @pl.when(p == 0)
    def _():
        m_scr[...] = jnp.full(m_scr.shape, NEG_BIG, _F32)
        l_scr[...] = jnp.zeros(l_scr.shape, _F32)
        accd_scr[...] = jnp.zeros(accd_scr.shape, _F32)
        c_scr[...] = jnp.zeros(c_scr.shape, _F32)
        accs_scr[...] = jnp.zeros(accs_scr.shape, _F32)
        knp_scr[...] = jnp.zeros(knp_scr.shape, _BF16)
        vnp_scr[...] = jnp.zeros(vnp_scr.shape, _BF16)
        for h in range(n_heads):
            knp_scr[h, 0:HEAD_ROWS, :] = kn_ref[:, h * HEAD_DIM:(h + 1) * HEAD_DIM]
            vnp_scr[h, 0:HEAD_ROWS, :] = vn_ref[:, h * HEAD_DIM:(h + 1) * HEAD_DIM]
        process(lambda h: knp_scr[h], lambda h: vnp_scr[h], True)

    process(lambda h: kc_ref[:, h, :].astype(_BF16), lambda h: vc_ref[:, h, :].astype(_BF16), False)

    @pl.when(p == pl.num_programs(1) - 1)
    def _():
        o = accd_scr[...] / l_scr[...]
        lam = _lambda_full(lam_ref, lam0)
        od = o - lam * pltpu.roll(o, rd - ts, 0)
        ms = jnp.mean(od * od, axis=-1, keepdims=True)
        od_ref[...] = od * lax.rsqrt(ms + SUBLN_EPS) * subw_ref[...] * (1.0 - lam0)
        os_ref[...] = accs_scr[...]


def _sample_attn(page_table, lam_p, subw, q_rows, kn16, vn16, cache_k, cache_v, layer, lam0, n_diff, n_sb, ts):
    bs, n_pages = page_table.shape
    page = cache_k.shape[2]
    n_heads = n_diff + n_sb
    rows = n_heads * HEAD_ROWS
    rd, rs = n_diff * HEAD_ROWS, n_sb * HEAD_ROWS
    width = n_heads * HEAD_DIM

    def cache_map(b, p, pt):
        return (layer, pt[b * n_pages + (n_pages - 1 - p)], 0, 0, 0)

    grid_spec = pltpu.PrefetchScalarGridSpec(
        num_scalar_prefetch=1,
        grid=(bs, n_pages),
        in_specs=[pl.BlockSpec(lam_p.shape, lambda b, p, pt: (0, 0)),
                  pl.BlockSpec((1, HEAD_DIM), lambda b, p, pt: (0, 0)),
                  pl.BlockSpec((None, rows, HEAD_DIM), lambda b, p, pt: (b, 0, 0)),
                  pl.BlockSpec((None, HEAD_ROWS, width), lambda b, p, pt: (b, 0, 0)),
                  pl.BlockSpec((None, HEAD_ROWS, width), lambda b, p, pt: (b, 0, 0)),
                  pl.BlockSpec((None, None, page, n_heads, HEAD_DIM), cache_map),
                  pl.BlockSpec((None, None, page, n_heads, HEAD_DIM), cache_map)],
        out_specs=[pl.BlockSpec((None, rd, HEAD_DIM), lambda b, p, pt: (b, 0, 0)),
                   pl.BlockSpec((None, rs, HEAD_DIM), lambda b, p, pt: (b, 0, 0))],
        scratch_shapes=[pltpu.VMEM((n_heads, page, HEAD_DIM), _BF16), pltpu.VMEM((n_heads, page, HEAD_DIM), _BF16),
                        pltpu.VMEM((rd, 1), _F32), pltpu.VMEM((rd, 1), _F32), pltpu.VMEM((rd, HEAD_DIM), _F32),
                        pltpu.VMEM((rs, 1), _F32), pltpu.VMEM((rs, HEAD_DIM), _F32)])
    return pl.pallas_call(
        functools.partial(_sample_attn_kernel, lam0=lam0, n_diff=n_diff, n_sb=n_sb, ts=ts, page=page),
        grid_spec=grid_spec,
        out_shape=[jax.ShapeDtypeStruct((bs, rd, HEAD_DIM), _F32), jax.ShapeDtypeStruct((bs, rs, HEAD_DIM), _F32)],
        compiler_params=_params("parallel", "arbitrary"),
        name="sample_attn",
    )(page_table.reshape(-1), lam_p, subw, q_rows, kn16, vn16, cache_k, cache_v)


def _sample_query_rows(q_d, q_s, bs, ts, n_diff, n_sb):
    qd = q_d.reshape(bs, ts, n_diff, 2, DK_DIFF)
    z = jnp.zeros_like(qd[..., 0, :])
    c0 = jnp.concatenate([qd[..., 0, :], z], axis=-1)
    c1 = jnp.concatenate([z, qd[..., 1, :]], axis=-1)
    dr = jnp.concatenate([c0, c1], axis=1)
    dr = jnp.transpose(dr, (0, 2, 1, 3))
    dr = jnp.pad(dr, ((0, 0), (0, 0), (0, HEAD_ROWS - 2 * ts), (0, 0)))
    sr = jnp.transpose(q_s.reshape(bs, ts, n_sb, HEAD_DIM), (0, 2, 1, 3))
    sr = jnp.pad(sr, ((0, 0), (0, 0), (0, HEAD_ROWS - ts), (0, 0)))
    return jnp.concatenate([dr, sr], axis=1).reshape(bs, (n_diff + n_sb) * HEAD_ROWS, HEAD_DIM)


def _branch_kernel(d_ref, sgp_ref, od_ref, os_ref, sga_ref, gate_ref, pw_ref, ps_ref, wb_ref, o_ref, *, pg):
    p = d_ref.shape[1]
    dm = o_ref.shape[1]
    d = d_ref[...]
    ys = [jnp.dot(d[:, g * pg:(g + 1) * pg].astype(_BF16), pw_ref[g], preferred_element_type=_F32)
          for g in range(len(POOL_WINDOWS))]
    y_pool = jnp.concatenate(ys, axis=1) * ps_ref[...]
    b0 = (y_pool * sgp_ref[...]).astype(_BF16)
    b1 = (od_ref[...] * sga_ref[:, :p]).astype(_BF16)
    b2 = (os_ref[...] * sga_ref[:, p:]).astype(_BF16)
    merged = gate_ref[:, 0:dm] * jnp.dot(b0, wb_ref[0], preferred_element_type=_F32)
    merged += gate_ref[:, dm:2 * dm] * jnp.dot(b1, wb_ref[1], preferred_element_type=_F32)
    merged += gate_ref[:, 2 * dm:3 * dm] * jnp.dot(b2, wb_ref[2], preferred_element_type=_F32)
    o_ref[...] = merged.astype(_BF16)


def _branch_merge(d, sg_pool, od, os_, sg_att, gates, pool_w16, pool_scale, w_branch16, tm):
    m, p = d.shape
    dm = w_branch16.shape[2]
    pg = p // len(POOL_WINDOWS)
    row = lambda w: pl.BlockSpec((tm, w), lambda i: (i, 0))
    const = lambda shape: pl.BlockSpec(shape, lambda i: (0,) * len(shape), pipeline_mode=pl.Buffered(1))
    return pl.pallas_call(
        functools.partial(_branch_kernel, pg=pg),
        grid=(m // tm,),
        in_specs=[row(p), row(p), row(p), row(p), row(2 * p), row(3 * dm),
                  const(pool_w16.shape), const((1, p)), const(w_branch16.shape)],
        out_specs=row(dm),
        out_shape=jax.ShapeDtypeStruct((m, dm), _BF16),
        compiler_params=_params("parallel"),
        name="branch_merge",
    )(d, sg_pool, od, os_, sg_att, gates, pool_w16, pool_scale.reshape(1, p), w_branch16)


def _out_kernel(mg_ref, wo_ref, x_ref, wn_ref, xo_ref, hn_ref):
    x = x_ref[...] + jnp.dot(mg_ref[...], wo_ref[...], preferred_element_type=_F32)
    xo_ref[...] = x
    ms = jnp.mean(x * x, axis=-1, keepdims=True)
    hn_ref[...] = (x * lax.rsqrt(ms + EPS) * wn_ref[...]).astype(hn_ref.dtype)


def _out_proj(merged, w_out16, x, w_next, next_dtype, tm):
    m, dm = x.shape
    row = pl.BlockSpec((tm, dm), lambda i: (i, 0))
    return pl.pallas_call(
        _out_kernel,
        grid=(m // tm,),
        in_specs=[row, pl.BlockSpec((dm, dm), lambda i: (0, 0), pipeline_mode=pl.Buffered(1)), row,
                  pl.BlockSpec((1, dm), lambda i: (0, 0))],
        out_specs=[row, row],
        out_shape=[jax.ShapeDtypeStruct((m, dm), _F32), jax.ShapeDtypeStruct((m, dm), next_dtype)],
        compiler_params=_params("parallel"),
        name="out_proj",
    )(merged, w_out16, x, w_next.reshape(1, dm))


def _rope_tables(pos):
    half = DK_DIFF // 2
    inv = 1.0 / (ROPE_THETA ** (jnp.arange(half, dtype=_F32) * 2.0 / DK_DIFF))
    ang = pos.astype(_F32)[:, None] * inv[None, :]
    cos, sin = jnp.cos(ang), jnp.sin(ang)
    reps = HEAD_DIM // DK_DIFF
    return jnp.tile(jnp.concatenate([cos, cos], axis=1), (1, reps)), jnp.tile(jnp.concatenate([-sin, sin], axis=1), (1, reps))


def kernel(x_prompt, x_sample, cache_k, cache_v, state_pool, page_table, norm_w, w_in, gate_b, pool_w, pool_scale,
           diff_lambda, diff_subln_w, w_branch, w_out, norm_f_w):
    bp, tp, dm = x_prompt.shape
    bs, ts, _ = x_sample.shape
    assert bp == 1 and 2 * ts <= HEAD_ROWS
    depth = w_in.shape[0]
    n_pages = page_table.shape[1]
    page, n_heads = cache_k.shape[2], cache_k.shape[3]
    assert cache_k.shape[4] == HEAD_DIM and state_pool.shape[2] == POOL_BUF
    ms = bs * ts
    m = tp + ms
    bw = dm // 2
    n_diff = bw // HEAD_DIM
    n_sb = bw // HEAD_DIM
    assert n_heads == n_diff + n_sb
    kvw = n_heads * HEAD_DIM
    offs = {}
    o = 0
    for name, size in (("u", bw), ("g_pool", bw), ("q_d", bw), ("q_s", bw), ("k", kvw), ("v", kvw),
                       ("g_att", 2 * bw), ("g_br", 3 * dm)):
        offs[name] = o
        o += size
    assert o == w_in.shape[2]

    tm = _pow2_tile(512, tp, ms)
    tn = _pow2_tile(1024, bw)
    tq = _pow2_tile(256, tp)
    tmb = _pow2_tile(256, tp, ms)

    pos = jnp.concatenate([jnp.arange(tp, dtype=jnp.int32),
                           jnp.tile(n_pages * page + jnp.arange(ts, dtype=jnp.int32), bs)])
    cos_t, sin_t = _rope_tables(pos)

    x = jnp.concatenate([x_prompt.reshape(tp, dm), x_sample.reshape(ms, dm)], axis=0)
    h = _rmsnorm(x, norm_w[0], _BF16, tm)

    k_p, v_p, pool_p, k_s, v_s, pool_s = [], [], [], [], [], []
    y = None
    for l in range(depth):
        lam0 = _lambda_init(l)
        w16 = w_in[l].astype(_BF16)
        proj = functools.partial(_proj, h, w16, tm=tm, tn=tn)
        u = proj(offs["u"], bw, "plain")
        sg_pool = proj(offs["g_pool"], bw, "silu")
        q_d = proj(offs["q_d"], bw, "rope_scale_bf16", extras=(cos_t, sin_t), scale=DK_DIFF ** -0.5)
        q_s = proj(offs["q_s"], bw, "scale_bf16", scale=HEAD_DIM ** -0.5)
        k32, k16 = proj(offs["k"], kvw, "key", extras=(cos_t, sin_t))
        v32, v16 = proj(offs["v"], kvw, "value")
        sg_att = proj(offs["g_att"], 2 * bw, "silu")
        gates = proj(offs["g_br"], 3 * dm, "gate", extras=(gate_b[l].reshape(1, 3 * dm),))

        d = _pool_prompt(u, tp, tm)
        u_s = u[tp:].reshape(bs, ts, bw)
        d_s = _pool_sample(jnp.transpose(state_pool[l], (1, 0, 2)), jnp.transpose(u_s, (1, 0, 2)))
        d = lax.dynamic_update_slice(d, jnp.transpose(d_s, (1, 0, 2)).reshape(ms, bw), (tp, 0))

        lam_p = diff_lambda[l]
        subw = diff_subln_w[l].reshape(1, HEAD_DIM)
        od = _diff_prompt(lam_p, subw, q_d, k16, v16, tp, n_diff, m, lam0, tq)
        os_ = _sb_prompt(q_s, k16, v16, tp, n_sb, n_diff, m, tq)
        q_rows = _sample_query_rows(q_d[tp:], q_s[tp:], bs, ts, n_diff, n_sb)
        pad = ((0, 0), (0, HEAD_ROWS - ts), (0, 0))
        kn16 = jnp.pad(k16[tp:].reshape(bs, ts, kvw), pad)
        vn16 = jnp.pad(v16[tp:].reshape(bs, ts, kvw), pad)
        od_s, os_s = _sample_attn(page_table, lam_p, subw, q_rows, kn16, vn16, cache_k, cache_v, l, lam0,
                                  n_diff, n_sb, ts)
        od_s = jnp.transpose(od_s.reshape(bs, n_diff, HEAD_ROWS, HEAD_DIM)[:, :, :ts], (0, 2, 1, 3))
        os_s = jnp.transpose(os_s.reshape(bs, n_sb, HEAD_ROWS, HEAD_DIM)[:, :, :ts], (0, 2, 1, 3))
        od = lax.dynamic_update_slice(od, od_s.reshape(ms, bw), (tp, 0))
        os_ = lax.dynamic_update_slice(os_, os_s.reshape(ms, bw), (tp, 0))

        merged = _branch_merge(d, sg_pool, od, os_, sg_att, gates, pool_w[l].astype(_BF16), pool_scale[l],
                               w_branch[l].astype(_BF16), tmb)
        last = l == depth - 1
        w_next = norm_f_w if last else norm_w[l + 1]
        x, hn = _out_proj(merged, w_out[l].astype(_BF16), x, w_next, _F32 if last else _BF16, tmb)
        if last:
            y = hn
        else:
            h = hn

        k_p.append(k32[:tp].reshape(1, tp, n_heads, HEAD_DIM))
        v_p.append(v32[:tp].reshape(1, tp, n_heads, HEAD_DIM))
        pool_p.append(u[tp - POOL_BUF:tp].reshape(1, POOL_BUF, bw))
        k_s.append(k32[tp:].reshape(bs, ts, n_heads, HEAD_DIM))
        v_s.append(v32[tp:].reshape(bs, ts, n_heads, HEAD_DIM))
        pool_s.append(jnp.concatenate([state_pool[l], u_s], axis=1)[:, -POOL_BUF:])

    return (y[:tp].reshape(1, tp, dm), y[tp:].reshape(bs, ts, dm), jnp.stack(k_p), jnp.stack(v_p),
            jnp.stack(pool_p), jnp.stack(k_s), jnp.stack(v_s), jnp.stack(pool_s))
```

```python
import functools
import math

import jax
import jax.numpy as jnp
from jax import lax
from jax.experimental import pallas as pl
from jax.experimental.pallas import tpu as pltpu

POOL_WINDOWS = (2, 4, 8, 16)
POOL_BUF = max(POOL_WINDOWS) - 1
POOL_HALO = 16
HEAD_DIM = 128
DK_DIFF = HEAD_DIM // 2
ROPE_THETA = 10000.0
EPS = 1e-6
SUBLN_EPS = 1e-5
NEG_BIG = -1e30
SB_CUTOFF = -104.0
LOG2E = math.log2(math.e)
PAGES_PER_STEP = 4
VMEM_LIMIT_BYTES = 56 * 1024 * 1024

_BF16 = jnp.bfloat16
_F32 = jnp.float32


def _lambda_init(layer):
    return 0.8 - 0.6 * math.exp(-0.3 * layer)


def _params(*sem):
    return pltpu.CompilerParams(dimension_semantics=sem, vmem_limit_bytes=VMEM_LIMIT_BYTES)


def _pow2_tile(cap, *sizes):
    t = cap
    while any(s % t for s in sizes):
        t //= 2
    return t


def _prompt_rows(i, ntp):
    return jnp.minimum(i, ntp - 1)


def _sample_rows(i, ntp):
    return jnp.maximum(i - ntp, 0)


def _rms_scale(x, w, eps):
    ms = jnp.mean(x * x, axis=-1, keepdims=True)
    return x * lax.rsqrt(ms + eps) * w


def _rmsnorm_kernel(xp_ref, xs_ref, w_ref, o_ref, *, ntp):
    x = jnp.where(pl.program_id(0) < ntp, xp_ref[...], xs_ref[...])
    o_ref[...] = _rms_scale(x, w_ref[...], EPS).astype(o_ref.dtype)


def _rmsnorm(xp, xs, w, out_dtype, tm):
    (tp, d), ms = xp.shape, xs.shape[0]
    ntp = tp // tm
    return pl.pallas_call(
        functools.partial(_rmsnorm_kernel, ntp=ntp),
        grid=((tp + ms) // tm,),
        in_specs=[pl.BlockSpec((tm, d), lambda i: (_prompt_rows(i, ntp), 0)),
                  pl.BlockSpec((tm, d), lambda i: (_sample_rows(i, ntp), 0)),
                  pl.BlockSpec((1, d), lambda i: (0, 0))],
        out_specs=pl.BlockSpec((tm, d), lambda i: (i, 0)),
        out_shape=jax.ShapeDtypeStruct((tp + ms, d), out_dtype),
        compiler_params=_params("arbitrary"),
        name="rmsnorm",
    )(xp, xs, w.reshape(1, d))


def _rope_heads(z, cos, sin_signed):
    tm = z.shape[0]
    lane = lax.broadcasted_iota(jnp.int32, (tm, HEAD_DIM), 1)
    first_half = (lane % DK_DIFF) < (DK_DIFF // 2)
    outs = []
    for h in range(z.shape[1] // HEAD_DIM):
        zh = z[:, h * HEAD_DIM:(h + 1) * HEAD_DIM]
        partner = jnp.where(first_half, pltpu.roll(zh, HEAD_DIM - DK_DIFF // 2, 1), pltpu.roll(zh, DK_DIFF // 2, 1))
        outs.append(zh * cos + partner * sin_signed)
    return jnp.concatenate(outs, axis=1)


def _proj_kernel(*refs, epilogue, scale, ntp, aliased):
    h_ref, w_ref, w16_scr = refs[0], refs[1], refs[-1]
    refs = refs[2:-1]
    j, i = pl.program_id(0), pl.program_id(1)

    @pl.when(i == 0)
    def _():
        w16_scr[...] = w_ref[...].astype(_BF16)

    z = jnp.dot(h_ref[...], w16_scr[...], preferred_element_type=_F32)

    def store_kv(val, op_ref, os_ref, o16_ref):
        o16_ref[...] = val.astype(_BF16)

        @pl.when(i < ntp)
        def _():
            op_ref[...] = val

        @pl.when(i >= ntp)
        def _():
            os_ref[...] = val

    if epilogue == "plain":
        refs[0][...] = z
    elif epilogue == "silu":
        refs[0][...] = z * jax.nn.sigmoid(z)
    elif epilogue == "scale_bf16":
        refs[0][...] = (z * scale).astype(_BF16)
    elif epilogue == "rope_scale_bf16":
        cos_ref, sin_ref, o_ref = refs
        o_ref[...] = (_rope_heads(z, cos_ref[...], sin_ref[...]) * scale).astype(_BF16)
    elif epilogue == "key":
        cos_ref, sin_ref = refs[:2]
        outs = refs[3:] if aliased else refs[2:]

        @pl.when(j == 0)
        def _():
            store_kv(_rope_heads(z, cos_ref[...], sin_ref[...]), *outs)

        @pl.when(j != 0)
        def _():
            store_kv(z, *outs)
    elif epilogue == "value":
        store_kv(z, *(refs[1:] if aliased else refs))
    elif epilogue == "gate":
        b_ref, o_ref = refs
        o_ref[...] = jax.nn.sigmoid(z + b_ref[...])
    else:
        raise ValueError(epilogue)


def _proj(h, w_all, layer, col_off, n_cols, epilogue, *, tm, tn, tp, extras=(), scale=1.0, stack=None):
    m, d = h.shape
    depth = w_all.shape[0]
    assert col_off % tn == 0 and n_cols % tn == 0 and tp % tm == 0
    off_blk = col_off // tn
    ntp = tp // tm
    grid = (n_cols // tn, m // tm)
    in_specs = [pl.BlockSpec((tm, d), lambda j, i: (i, 0)),
                pl.BlockSpec((None, d, tn), lambda j, i: (layer, 0, off_blk + j))]
    row_tab = pl.BlockSpec((tm, HEAD_DIM), lambda j, i: (i, 0))
    out_tile = pl.BlockSpec((tm, tn), lambda j, i: (i, j))
    f32_out = jax.ShapeDtypeStruct((m, n_cols), _F32)
    b16_out = jax.ShapeDtypeStruct((m, n_cols), _BF16)
    kv_specs = [pl.BlockSpec((None, tm, tn), lambda j, i: (layer, _prompt_rows(i, ntp), j)),
                pl.BlockSpec((tm, tn), lambda j, i: (_sample_rows(i, ntp), j)), out_tile]
    kv_shapes = [jax.ShapeDtypeStruct((depth, tp, n_cols), _F32), jax.ShapeDtypeStruct((m - tp, n_cols), _F32), b16_out]
    aliases = {}
    if epilogue in ("plain", "silu"):
        out_specs, out_shape = out_tile, f32_out
    elif epilogue == "scale_bf16":
        out_specs, out_shape = out_tile, b16_out
    elif epilogue == "rope_scale_bf16":
        in_specs += [row_tab, row_tab]
        out_specs, out_shape = out_tile, b16_out
    elif epilogue in ("key", "value"):
        if epilogue == "key":
            assert n_cols == 2 * tn
            in_specs += [row_tab, row_tab]
        out_specs, out_shape = kv_specs, kv_shapes
        if stack is not None:
            in_specs += [pl.BlockSpec(memory_space=pl.ANY)]
            extras = tuple(extras) + (stack,)
            aliases = {len(in_specs) - 1: 0}
    elif epilogue == "gate":
        in_specs += [pl.BlockSpec((1, tn), lambda j, i: (0, j))]
        out_specs, out_shape = out_tile, f32_out
    else:
        raise ValueError(epilogue)
    return pl.pallas_call(
        functools.partial(_proj_kernel, epilogue=epilogue, scale=scale, ntp=ntp, aliased=bool(aliases)),
        grid=grid, in_specs=in_specs, out_specs=out_specs, out_shape=out_shape,
        scratch_shapes=[pltpu.VMEM((d, tn), _BF16)],
        input_output_aliases=aliases,
        compiler_params=_params("arbitrary", "arbitrary"),
        name="proj_" + epilogue,
    )(h, w_all, *extras)


def _pool_prompt_kernel(u_ref, prev_ref, d_ref, *, tm, pg):
    i = pl.program_id(0)
    u = u_ref[...]
    prev = jnp.where(i > 0, prev_ref[...], 0.0)
    ext = jnp.concatenate([prev, u], axis=0)
    t = i * tm + lax.broadcasted_iota(jnp.int32, (tm, 1), 0)
    for g, w in enumerate(POOL_WINDOWS):
        s = ext[:, g * pg:(g + 1) * pg]
        sh = 1
        while sh < w:
            s = s + pltpu.roll(s, sh, 0)
            sh *= 2
        cnt = jnp.minimum(w, t + 1).astype(_F32)
        d_ref[:, g * pg:(g + 1) * pg] = s[POOL_HALO:] / cnt - u[:, g * pg:(g + 1) * pg]


def _pool_prompt(u, tp, tm):
    m, p = u.shape
    pg = p // len(POOL_WINDOWS)
    hb = tm // POOL_HALO
    return pl.pallas_call(
        functools.partial(_pool_prompt_kernel, tm=tm, pg=pg),
        grid=(tp // tm,),
        in_specs=[pl.BlockSpec((tm, p), lambda i: (i, 0)),
                  pl.BlockSpec((POOL_HALO, p), lambda i: (jnp.maximum(i * hb - 1, 0), 0))],
        out_specs=pl.BlockSpec((tm, p), lambda i: (i, 0)),
        out_shape=jax.ShapeDtypeStruct((m, p), _F32),
        compiler_params=_params("parallel"),
        name="pool_prompt",
    )(u, u)


def _pool_sample_kernel(st_ref, un_ref, d_ref, *, pg):
    nb, ts = st_ref.shape[0], un_ref.shape[0]
    for g, w in enumerate(POOL_WINDOWS):
        cols = slice(g * pg, (g + 1) * pg)
        rows = [st_ref[r, :, cols] for r in range(nb)] + [un_ref[t, :, cols] for t in range(ts)]
        for t in range(ts):
            r = nb + t
            lo = max(r + 1 - w, 0)
            s = rows[lo]
            for rr in range(lo + 1, r + 1):
                s = s + rows[rr]
            d_ref[t, :, cols] = s / float(r + 1 - lo) - rows[r]


def _pool_sample(state_t, u_new_t):
    nb, bs, p = state_t.shape
    ts = u_new_t.shape[0]
    pg = p // len(POOL_WINDOWS)
    return pl.pallas_call(
        functools.partial(_pool_sample_kernel, pg=pg),
        grid=(1,),
        in_specs=[pl.BlockSpec((nb, bs, p), lambda i: (0, 0, 0)), pl.BlockSpec((ts, bs, p), lambda i: (0, 0, 0))],
        out_specs=pl.BlockSpec((ts, bs, p), lambda i: (0, 0, 0)),
        out_shape=jax.ShapeDtypeStruct((ts, bs, p), _F32),
        compiler_params=_params("arbitrary"),
        name="pool_sample",
    )(state_t, u_new_t)


def _lambda_full(lam_ref, lam0):
    dl = lam_ref[...]
    a = jnp.sum(dl[0:1] * dl[1:2], axis=-1, keepdims=True)
    b = jnp.sum(dl[2:3] * dl[3:4], axis=-1, keepdims=True)
    return jnp.exp(a) - jnp.exp(b) + lam0


def _dot_nt(a, b):
    return lax.dot_general(a, b, (((1,), (1,)), ((), ())), preferred_element_type=_F32)


def _softmax_init(m_scr, l_scr, acc_scr):
    m_scr[...] = jnp.full(m_scr.shape, NEG_BIG, _F32)
    l_scr[...] = jnp.zeros(l_scr.shape, _F32)
    acc_scr[...] = jnp.zeros(acc_scr.shape, _F32)


def _softmax_update(s_parts, v_parts, m_scr, l_scr, acc_scr, valid=None):
    part_chunks = []
    for s in s_parts:
        chunks = [s[:, c * HEAD_DIM:(c + 1) * HEAD_DIM] for c in range(s.shape[1] // HEAD_DIM)]
        if valid is not None:
            chunks = [jnp.where(valid, c, NEG_BIG) for c in chunks]
        part_chunks.append(chunks)
    flat = [c for chunks in part_chunks for c in chunks]
    mx = flat[0]
    for c in flat[1:]:
        mx = jnp.maximum(mx, c)
    m_old = m_scr[...]
    m_new = jnp.maximum(m_old, jnp.max(mx, axis=-1, keepdims=True))
    alpha = jnp.exp2(m_old - m_new)
    acc = alpha * acc_scr[...]
    lsum = alpha * l_scr[...]
    for chunks, v16 in zip(part_chunks, v_parts):
        ps = [jnp.exp2(c - m_new) for c in chunks]
        for p in ps:
            lsum = lsum + p
        p16 = jnp.concatenate([p.astype(_BF16) for p in ps], axis=1)
        acc = acc + jnp.dot(p16, v16, preferred_element_type=_F32)
    l_scr[...] = lsum
    acc_scr[...] = acc
    m_scr[...] = m_new


def _softmax_result(l_scr, acc_scr):
    return acc_scr[...] / jnp.sum(l_scr[...], axis=-1, keepdims=True)


def _diff_prompt_kernel(lam_ref, subw_ref, q_ref, k_ref, v_ref, o_ref, m_scr, l_scr, acc_scr, *, lam0, tq):
    i = pl.program_id(1)
    q = q_ref[...]
    lane = lax.broadcasted_iota(jnp.int32, q.shape, 1)
    zero = jnp.zeros_like(q)
    qq = jnp.concatenate([jnp.where(lane < DK_DIFF, q, zero), jnp.where(lane >= DK_DIFF, q, zero)], axis=0)
    _softmax_init(m_scr, l_scr, acc_scr)

    def block(j, masked):
        start = pl.multiple_of(j * tq, tq)
        s = _dot_nt(qq, k_ref[pl.ds(start, tq), :])
        if masked:
            row = lax.broadcasted_iota(jnp.int32, s.shape, 0) % tq
            col = lax.broadcasted_iota(jnp.int32, s.shape, 1)
            s = jnp.where(col <= row, s, NEG_BIG)
        _softmax_update([s], [v_ref[pl.ds(start, tq), :]], m_scr, l_scr, acc_scr)

    def body(j, carry):
        block(j, False)
        return carry

    lax.fori_loop(0, i, body, 0)
    block(i, True)

    o = _softmax_result(l_scr, acc_scr)
    lam = _lambda_full(lam_ref, lam0)
    od = o[:tq] - lam * o[tq:]
    ms = jnp.mean(od * od, axis=-1, keepdims=True)
    o_ref[...] = od * lax.rsqrt(ms + SUBLN_EPS) * subw_ref[...] * (1.0 - lam0)


def _diff_prompt(lam_p, subw, q_d, k16, v16, tp, n_heads, m_rows, lam0, tq):
    return pl.pallas_call(
        functools.partial(_diff_prompt_kernel, lam0=lam0, tq=tq),
        grid=(n_heads, tp // tq),
        in_specs=[pl.BlockSpec(lam_p.shape, lambda h, i: (0, 0)),
                  pl.BlockSpec((1, HEAD_DIM), lambda h, i: (0, 0)),
                  pl.BlockSpec((tq, HEAD_DIM), lambda h, i: (i, h)),
                  pl.BlockSpec((tp, HEAD_DIM), lambda h, i: (0, h)),
                  pl.BlockSpec((tp, HEAD_DIM), lambda h, i: (0, h))],
        out_specs=pl.BlockSpec((tq, HEAD_DIM), lambda h, i: (i, h)),
        out_shape=jax.ShapeDtypeStruct((m_rows, n_heads * HEAD_DIM), _F32),
        scratch_shapes=[pltpu.VMEM((2 * tq, HEAD_DIM), _F32)] * 3,
        compiler_params=_params("parallel", "arbitrary"),
        name="diff_prompt",
    )(lam_p, subw, q_d, k16, v16)


def _log_sigmoid_pair(s):
    t = jnp.log1p(jnp.exp(-jnp.abs(s)))
    return jnp.minimum(s, 0.0) - t, -jnp.maximum(s, 0.0) - t


def _suffix_sums(lr, tri):
    hi = lr.astype(_BF16)
    lo = (lr - hi.astype(_F32)).astype(_BF16)
    return (jnp.dot(hi, tri, preferred_element_type=_F32) + jnp.dot(lo, tri, preferred_element_type=_F32))


def _strict_lower_ones(n):
    r = lax.broadcasted_iota(jnp.int32, (n, n), 0)
    c = lax.broadcasted_iota(jnp.int32, (n, n), 1)
    return (r > c).astype(_BF16)


def _sb_prompt_kernel(q_ref, k_ref, v_ref, o_ref, c_scr, acc_scr, *, tq):
    i = pl.program_id(1)
    q = q_ref[...]
    tri = _strict_lower_ones(tq)
    c_scr[...] = jnp.zeros(c_scr.shape, _F32)
    acc_scr[...] = jnp.zeros(acc_scr.shape, _F32)

    def block(j, masked):
        start = pl.multiple_of(j * tq, tq)
        kj = k_ref[pl.ds(start, tq), :]
        vj = v_ref[pl.ds(start, tq), :]
        s = _dot_nt(q, kj)
        lb, lr = _log_sigmoid_pair(s)
        if masked:
            row = lax.broadcasted_iota(jnp.int32, s.shape, 0)
            col = lax.broadcasted_iota(jnp.int32, s.shape, 1)
            valid = col < row
            lr = jnp.where(valid, lr, 0.0)
        c = c_scr[...]
        a = jnp.exp(lb + _suffix_sums(lr, tri) + c)
        if masked:
            a = jnp.where(valid, a, 0.0)
        acc_scr[...] += jnp.dot(a.astype(_BF16), vj, preferred_element_type=_F32)
        c_scr[...] = c + jnp.sum(lr, axis=-1, keepdims=True)

    block(i, True)

    def cond(j):
        return jnp.logical_and(j >= 0, jnp.max(c_scr[...]) > SB_CUTOFF)

    def body(j):
        block(j, False)
        return j - 1

    lax.while_loop(cond, body, i - 1)
    o_ref[...] = acc_scr[...]


def _sb_prompt(q_s, k16, v16, tp, n_heads, head_off, m_rows, tq):
    return pl.pallas_call(
        functools.partial(_sb_prompt_kernel, tq=tq),
        grid=(n_heads, tp // tq),
        in_specs=[pl.BlockSpec((tq, HEAD_DIM), lambda h, i: (i, h)),
                  pl.BlockSpec((tp, HEAD_DIM), lambda h, i: (0, head_off + h)),
                  pl.BlockSpec((tp, HEAD_DIM), lambda h, i: (0, head_off + h))],
        out_specs=pl.BlockSpec((tq, HEAD_DIM), lambda h, i: (i, h)),
        out_shape=jax.ShapeDtypeStruct((m_rows, n_heads * HEAD_DIM), _F32),
        scratch_shapes=[pltpu.VMEM((tq, 1), _F32), pltpu.VMEM((tq, HEAD_DIM), _F32)],
        compiler_params=_params("parallel", "arbitrary"),
        name="sb_prompt",
    )(q_s, k16, v16)


def _sample_attn_kernel(pt_ref, lam_ref, subw_ref, tmat_ref, qd_ref, qs_ref, kn_ref, vn_ref, *rest, lam0, ts, hg, gp):
    del pt_ref
    kc_refs, vc_refs = rest[:gp], rest[gp:2 * gp]
    od_ref, os_ref, m_scr, l_scr, accd_scr, run_scr, accs_scr = rest[2 * gp:]
    p = pl.program_id(1)
    qd = qd_ref[...]
    qs = qs_ref[...]
    rd, rs = qd.shape[0], qs.shape[0]
    lane_d = lax.broadcasted_iota(jnp.int32, (rd, HEAD_DIM), 1)
    row_d = lax.broadcasted_iota(jnp.int32, (rd, HEAD_DIM), 0)
    lane_s = lax.broadcasted_iota(jnp.int32, (rs, HEAD_DIM), 1)
    row_s = lax.broadcasted_iota(jnp.int32, (rs, HEAD_DIM), 0)
    own_d = (lane_d % hg) == (row_d // (2 * ts))
    own_s = (lane_s % hg) == (row_s // ts)

    def rows_of(ref, group):
        x = ref[:, group * hg:(group + 1) * hg, :]
        return x.reshape(x.shape[0] * hg, HEAD_DIM).astype(_BF16)

    def diff_block(k_refs, v_refs, valid):
        s_parts = [_dot_nt(qd, rows_of(k_ref, 0)) for k_ref in k_refs]
        _softmax_update(s_parts, [rows_of(v_ref, 0) for v_ref in v_refs], m_scr, l_scr, accd_scr, valid=valid)

    def sb_block(k_ref, v_ref, valid):
        s = _dot_nt(qs, rows_of(k_ref, 1))
        nch = s.shape[1] // HEAD_DIM
        lbs, lrs = [], []
        for c in range(nch):
            lb, lr = _log_sigmoid_pair(s[:, c * HEAD_DIM:(c + 1) * HEAD_DIM])
            lbs.append(lb)
            lrs.append(jnp.where(valid, lr, 0.0))
        lr_all = jnp.concatenate(lrs, axis=0)
        hi = lr_all.astype(_BF16)
        lo = (lr_all - hi.astype(_F32)).astype(_BF16)
        tm = tmat_ref[...]
        r = jnp.dot(hi, tm, preferred_element_type=_F32) + jnp.dot(lo, tm, preferred_element_type=_F32)
        run = run_scr[...]
        a_parts = [None] * nch
        for c in reversed(range(nch)):
            rc = r[c * rs:(c + 1) * rs]
            a = jnp.exp(lbs[c] + rc[:, :HEAD_DIM] + run)
            a_parts[c] = jnp.where(valid, a, 0.0).astype(_BF16)
            run = run + rc[:, HEAD_DIM:]
        run_scr[...] = run
        accs_scr[...] += jnp.dot(jnp.concatenate(a_parts, axis=1), rows_of(v_ref, 1), preferred_element_type=_F32)

    @pl.when(p == 0)
    def _():
        _softmax_init(m_scr, l_scr, accd_scr)
        run_scr[...] = jnp.zeros(run_scr.shape, _F32)
        accs_scr[...] = jnp.zeros(accs_scr.shape, _F32)
        diff_block([kn_ref], [vn_ref],
                   jnp.logical_and(own_d, jnp.logical_and(lane_d // hg < ts, lane_d // hg <= row_d % ts)))
        sb_block(kn_ref, vn_ref, jnp.logical_and(own_s, jnp.logical_and(lane_s // hg < ts, lane_s // hg < row_s % ts)))

    diff_block(kc_refs, vc_refs, own_d)

    def sb_pages(g):
        @pl.when(jnp.max(jnp.where(own_s, run_scr[...], NEG_BIG)) > SB_CUTOFF)
        def _():
            sb_block(kc_refs[g], vc_refs[g], own_s)
            if g + 1 < gp:
                sb_pages(g + 1)

    sb_pages(0)

    @pl.when(p == pl.num_programs(1) - 1)
    def _():
        o = _softmax_result(l_scr, accd_scr)
        lam = _lambda_full(lam_ref, lam0)
        od = o - lam * pltpu.roll(o, rd - ts, 0)
        ms = jnp.mean(od * od, axis=-1, keepdims=True)
        od_ref[...] = od * lax.rsqrt(ms + SUBLN_EPS) * subw_ref[...] * (1.0 - lam0)
        os_ref[...] = accs_scr[...]


def _suffix_matrix(hg):
    idx = jnp.arange(HEAD_DIM)
    tok, head = idx // hg, idx % hg
    same = head[:, None] == head[None, :]
    later = jnp.logical_and(same, tok[:, None] > tok[None, :])
    return jnp.concatenate([later, same], axis=1).astype(_BF16)


def _sample_attn(page_table, lam_p, subw, qd_rows, qs_rows, kn, vn, cache_k, cache_v, layer, lam0, ts):
    bs, n_pages = page_table.shape
    page, n_heads = cache_k.shape[2], cache_k.shape[3]
    hg = n_heads // 2
    rd, rs = qd_rows.shape[1], qs_rows.shape[1]
    assert hg == 8 and kn.shape[1] * hg == HEAD_DIM and (page * hg) % HEAD_DIM == 0

    gp = _pow2_tile(PAGES_PER_STEP, n_pages)

    def cache_spec(g):
        return pl.BlockSpec((None, None, page, n_heads, HEAD_DIM),
                            lambda b, p, pt: (layer, pt[b * n_pages + (n_pages - 1 - (p * gp + g))], 0, 0, 0))

    per_b = lambda shape: pl.BlockSpec((None,) + shape, lambda b, p, pt: (b,) + (0,) * len(shape))
    const = lambda shape: pl.BlockSpec(shape, lambda b, p, pt: (0,) * len(shape))
    grid_spec = pltpu.PrefetchScalarGridSpec(
        num_scalar_prefetch=1,
        grid=(bs, n_pages // gp),
        in_specs=[const(lam_p.shape), const((1, HEAD_DIM)), const((HEAD_DIM, 2 * HEAD_DIM)),
                  per_b((rd, HEAD_DIM)), per_b((rs, HEAD_DIM)),
                  per_b(kn.shape[1:]), per_b(vn.shape[1:])]
        + [cache_spec(g) for g in range(gp)] + [cache_spec(g) for g in range(gp)],
        out_specs=[per_b((rd, HEAD_DIM)), per_b((rs, HEAD_DIM))],
        scratch_shapes=[pltpu.VMEM((rd, HEAD_DIM), _F32)] * 3 + [pltpu.VMEM((rs, HEAD_DIM), _F32)] * 2)
    return pl.pallas_call(
        functools.partial(_sample_attn_kernel, lam0=lam0, ts=ts, hg=hg, gp=gp),
        grid_spec=grid_spec,
        out_shape=[jax.ShapeDtypeStruct((bs, rd, HEAD_DIM), _F32), jax.ShapeDtypeStruct((bs, rs, HEAD_DIM), _F32)],
        compiler_params=_params("parallel", "arbitrary"),
        name="sample_attn",
    )(page_table.reshape(-1), lam_p, subw, _suffix_matrix(hg), qd_rows, qs_rows, kn, vn,
      *([cache_k] * gp), *([cache_v] * gp))


def _sample_query_rows(q_d, q_s, bs, ts, n_diff, n_sb):
    qd = q_d.reshape(bs, ts, n_diff, 2, DK_DIFF)
    z = jnp.zeros_like(qd[..., 0, :])
    comps = jnp.stack([jnp.concatenate([qd[..., 0, :], z], axis=-1),
                       jnp.concatenate([z, qd[..., 1, :]], axis=-1)], axis=3)
    qd_rows = jnp.transpose(comps, (0, 2, 3, 1, 4)).reshape(bs, n_diff * 2 * ts, HEAD_DIM)
    qs_rows = jnp.transpose(q_s.reshape(bs, ts, n_sb, HEAD_DIM), (0, 2, 1, 3)).reshape(bs, n_sb * ts, HEAD_DIM)
    return qd_rows, qs_rows


def _branch_kernel(d_ref, sgp_ref, od_ref, os_ref, sga_ref, gate_ref, pw_ref, ps_ref, wb_ref, o_ref, *, pg):
    p = d_ref.shape[1]
    dm = o_ref.shape[1]
    d = d_ref[...]
    ys = [jnp.dot(d[:, g * pg:(g + 1) * pg].astype(_BF16), pw_ref[g], preferred_element_type=_F32)
          for g in range(len(POOL_WINDOWS))]
    y_pool = jnp.concatenate(ys, axis=1) * ps_ref[...]
    b0 = (y_pool * sgp_ref[...]).astype(_BF16)
    b1 = (od_ref[...] * sga_ref[:, :p]).astype(_BF16)
    b2 = (os_ref[...] * sga_ref[:, p:]).astype(_BF16)
    merged = gate_ref[:, 0:dm] * jnp.dot(b0, wb_ref[0], preferred_element_type=_F32)
    merged += gate_ref[:, dm:2 * dm] * jnp.dot(b1, wb_ref[1], preferred_element_type=_F32)
    merged += gate_ref[:, 2 * dm:3 * dm] * jnp.dot(b2, wb_ref[2], preferred_element_type=_F32)
    o_ref[...] = merged.astype(_BF16)


def _branch_merge(d, sg_pool, od, os_, sg_att, gates, pool_w16, pool_scale, w_branch16, tm):
    m, p = d.shape
    dm = w_branch16.shape[2]
    pg = p // len(POOL_WINDOWS)
    row = lambda w: pl.BlockSpec((tm, w), lambda i: (i, 0))
    const = lambda shape: pl.BlockSpec(shape, lambda i: (0,) * len(shape), pipeline_mode=pl.Buffered(1))
    return pl.pallas_call(
        functools.partial(_branch_kernel, pg=pg),
        grid=(m // tm,),
        in_specs=[row(p), row(p), row(p), row(p), row(2 * p), row(3 * dm),
                  const(pool_w16.shape), const((1, p)), const(w_branch16.shape)],
        out_specs=row(dm),
        out_shape=jax.ShapeDtypeStruct((m, dm), _BF16),
        compiler_params=_params("parallel"),
        name="branch_merge",
    )(d, sg_pool, od, os_, sg_att, gates, pool_w16, pool_scale.reshape(1, p), w_branch16)


def _out_kernel(*refs, x_pair, final, ntp):
    i = pl.program_id(0)
    mg_ref, wo_ref = refs[:2]
    if x_pair:
        x_in = jnp.where(i < ntp, refs[2][...], refs[3][...])
        refs = refs[4:]
    else:
        x_in = refs[2][...]
        refs = refs[3:]
    wn_ref = refs[0]
    x = x_in + jnp.dot(mg_ref[...], wo_ref[...], preferred_element_type=_F32)
    hn = _rms_scale(x, wn_ref[...], EPS)
    if final:
        yp_ref, ys_ref = refs[1:]

        @pl.when(i < ntp)
        def _():
            yp_ref[...] = hn

        @pl.when(i >= ntp)
        def _():
            ys_ref[...] = hn
    else:
        xo_ref, hn_ref = refs[1:]
        xo_ref[...] = x
        hn_ref[...] = hn.astype(hn_ref.dtype)


def _out_proj(merged, w_out16, x, w_next, tp, tm, final):
    m, dm = merged.shape
    ntp = tp // tm
    x_pair = isinstance(x, tuple)
    row = pl.BlockSpec((tm, dm), lambda i: (i, 0))
    p_rows = pl.BlockSpec((tm, dm), lambda i: (_prompt_rows(i, ntp), 0))
    s_rows = pl.BlockSpec((tm, dm), lambda i: (_sample_rows(i, ntp), 0))
    in_specs = [row, pl.BlockSpec((dm, dm), lambda i: (0, 0), pipeline_mode=pl.Buffered(1))]
    in_specs += [p_rows, s_rows] if x_pair else [row]
    in_specs += [pl.BlockSpec((1, dm), lambda i: (0, 0))]
    if final:
        out_specs = [p_rows, s_rows]
        out_shape = [jax.ShapeDtypeStruct((tp, dm), _F32), jax.ShapeDtypeStruct((m - tp, dm), _F32)]
    else:
        out_specs = [row, row]
        out_shape = [jax.ShapeDtypeStruct((m, dm), _F32), jax.ShapeDtypeStruct((m, dm), _BF16)]
    return pl.pallas_call(
        functools.partial(_out_kernel, x_pair=x_pair, final=final, ntp=ntp),
        grid=(m // tm,),
        in_specs=in_specs, out_specs=out_specs, out_shape=out_shape,
        compiler_params=_params("arbitrary"),
        name="out_proj",
    )(merged, w_out16, *(x if x_pair else (x,)), w_next.reshape(1, dm))


def _rope_tables(pos):
    half = DK_DIFF // 2
    inv = 1.0 / (ROPE_THETA ** (jnp.arange(half, dtype=_F32) * 2.0 / DK_DIFF))
    ang = pos.astype(_F32)[:, None] * inv[None, :]
    cos, sin = jnp.cos(ang), jnp.sin(ang)
    reps = HEAD_DIM // DK_DIFF
    return jnp.tile(jnp.concatenate([cos, cos], axis=1), (1, reps)), jnp.tile(jnp.concatenate([-sin, sin], axis=1), (1, reps))


def kernel(x_prompt, x_sample, cache_k, cache_v, state_pool, page_table, norm_w, w_in, gate_b, pool_w, pool_scale,
           diff_lambda, diff_subln_w, w_branch, w_out, norm_f_w):
    bp, tp, dm = x_prompt.shape
    bs, ts, _ = x_sample.shape
    assert bp == 1
    depth = w_in.shape[0]
    n_pages = page_table.shape[1]
    page, n_heads = cache_k.shape[2], cache_k.shape[3]
    assert cache_k.shape[4] == HEAD_DIM and state_pool.shape[2] == POOL_BUF
    ms = bs * ts
    m = tp + ms
    bw = dm // 2
    n_diff = bw // HEAD_DIM
    n_sb = bw // HEAD_DIM
    assert n_heads == n_diff + n_sb
    kvw = n_heads * HEAD_DIM
    offs = {}
    o = 0
    for name, size in (("u", bw), ("g_pool", bw), ("q_d", bw), ("q_s", bw), ("k", kvw), ("v", kvw),
                       ("g_att", 2 * bw), ("g_br", 3 * dm)):
        offs[name] = o
        o += size
    assert o == w_in.shape[2]

    tm = _pow2_tile(512, tp, ms)
    tn = _pow2_tile(1024, bw)
    tq = _pow2_tile(512, tp)
    tqs = _pow2_tile(256, tp)
    tmb = _pow2_tile(256, tp, ms)

    pos = jnp.concatenate([jnp.arange(tp, dtype=jnp.int32),
                           jnp.tile(n_pages * page + jnp.arange(ts, dtype=jnp.int32), bs)])
    cos_t, sin_t = _rope_tables(pos)

    x = (x_prompt.reshape(tp, dm), x_sample.reshape(ms, dm))
    h = _rmsnorm(*x, norm_w[0], _BF16, tm)

    kp_stack = vp_stack = y_p = y_s = None
    pool_p, k_s, v_s, pool_s = [], [], [], []
    for l in range(depth):
        lam0 = _lambda_init(l)
        proj = functools.partial(_proj, h, w_in, l, tm=tm, tn=tn, tp=tp)
        u = proj(offs["u"], bw, "plain")
        sg_pool = proj(offs["g_pool"], bw, "silu")
        q_d = proj(offs["q_d"], bw, "rope_scale_bf16", extras=(cos_t, sin_t), scale=DK_DIFF ** -0.5 * LOG2E)
        q_s = proj(offs["q_s"], bw, "scale_bf16", scale=HEAD_DIM ** -0.5)
        kp_stack, ks32, k16 = proj(offs["k"], kvw, "key", extras=(cos_t, sin_t), stack=kp_stack)
        vp_stack, vs32, v16 = proj(offs["v"], kvw, "value", stack=vp_stack)
        sg_att = proj(offs["g_att"], 2 * bw, "silu")
        gates = proj(offs["g_br"], 3 * dm, "gate", extras=(gate_b[l].reshape(1, 3 * dm),))

        d = _pool_prompt(u, tp, tm)
        u_s = u[tp:].reshape(bs, ts, bw)
        d_s = _pool_sample(jnp.transpose(state_pool[l], (1, 0, 2)), jnp.transpose(u_s, (1, 0, 2)))
        d = lax.dynamic_update_slice(d, jnp.transpose(d_s, (1, 0, 2)).reshape(ms, bw), (tp, 0))

        lam_p = diff_lambda[l]
        subw = diff_subln_w[l].reshape(1, HEAD_DIM)
        od = _diff_prompt(lam_p, subw, q_d, k16, v16, tp, n_diff, m, lam0, tq)
        os_ = _sb_prompt(q_s, k16, v16, tp, n_sb, n_diff, m, tqs)
        qd_rows, qs_rows = _sample_query_rows(q_d[tp:], q_s[tp:], bs, ts, n_diff, n_sb)
        new_tok = HEAD_DIM // (n_heads // 2)
        pad = ((0, 0), (0, new_tok - ts), (0, 0), (0, 0))
        kn = jnp.pad(ks32.reshape(bs, ts, n_heads, HEAD_DIM), pad)
        vn = jnp.pad(vs32.reshape(bs, ts, n_heads, HEAD_DIM), pad)
        od_s, os_s = _sample_attn(page_table, lam_p, subw, qd_rows, qs_rows, kn, vn, cache_k, cache_v, l, lam0, ts)
        od_s = jnp.transpose(od_s.reshape(bs, n_diff, 2 * ts, HEAD_DIM)[:, :, :ts], (0, 2, 1, 3))
        os_s = jnp.transpose(os_s.reshape(bs, n_sb, ts, HEAD_DIM), (0, 2, 1, 3))
        od = lax.dynamic_update_slice(od, od_s.reshape(ms, bw), (tp, 0))
        os_ = lax.dynamic_update_slice(os_, os_s.reshape(ms, bw), (tp, 0))

        merged = _branch_merge(d, sg_pool, od, os_, sg_att, gates, pool_w[l].astype(_BF16), pool_scale[l],
                               w_branch[l].astype(_BF16), tmb)
        last = l == depth - 1
        w_next = norm_f_w if last else norm_w[l + 1]
        res = _out_proj(merged, w_out[l].astype(_BF16), x, w_next, tp, tmb, last)
        if last:
            y_p, y_s = res
        else:
            x, h = res

        pool_p.append(u[tp - POOL_BUF:tp].reshape(1, POOL_BUF, bw))
        k_s.append(ks32.reshape(bs, ts, n_heads, HEAD_DIM))
        v_s.append(vs32.reshape(bs, ts, n_heads, HEAD_DIM))
        pool_s.append(jnp.concatenate([state_pool[l], u_s], axis=1)[:, -POOL_BUF:])

    return (y_p.reshape(1, tp, dm), y_s.reshape(bs, ts, dm),
            kp_stack.reshape(depth, 1, tp, n_heads, HEAD_DIM), vp_stack.reshape(depth, 1, tp, n_heads, HEAD_DIM),
            jnp.stack(pool_p), jnp.stack(k_s), jnp.stack(v_s), jnp.stack(pool_s))
```

```python
import functools
import math

import jax
import jax.numpy as jnp
from jax import lax
from jax.experimental import pallas as pl
from jax.experimental.pallas import tpu as pltpu

POOL_WINDOWS = (2, 4, 8, 16)
POOL_BUF = max(POOL_WINDOWS) - 1
POOL_HALO = 16
HEAD_DIM = 128
DK_DIFF = HEAD_DIM // 2
ROPE_THETA = 10000.0
EPS = 1e-6
SUBLN_EPS = 1e-5
NEG_BIG = -1e30
SB_CUTOFF = -104.0
LOG2E = math.log2(math.e)
PAGES_PER_STEP = 8
DIFF_Q_TILE = 1024
SB_Q_TILE = 256
SB_HEADS_PER_STEP = 2
SB_HEAD_PAGES = 2
SB_TAIL_PAGES_PER_STEP = 8
VMEM_LIMIT_BYTES = 56 * 1024 * 1024

_BF16 = jnp.bfloat16
_F32 = jnp.float32


def _lambda_init(layer):
    return 0.8 - 0.6 * math.exp(-0.3 * layer)


def _params(*sem):
    return pltpu.CompilerParams(dimension_semantics=sem, vmem_limit_bytes=VMEM_LIMIT_BYTES)


def _pow2_tile(cap, *sizes):
    t = cap
    while any(s % t for s in sizes):
        t //= 2
    return t


def _prompt_rows(i, ntp):
    return jnp.minimum(i, ntp - 1)


def _sample_rows(i, ntp):
    return jnp.maximum(i - ntp, 0)


def _rms_scale(x, w, eps):
    ms = jnp.mean(x * x, axis=-1, keepdims=True)
    return x * lax.rsqrt(ms + eps) * w


def _rmsnorm_kernel(xp_ref, xs_ref, w_ref, o_ref, *, ntp):
    x = jnp.where(pl.program_id(0) < ntp, xp_ref[...], xs_ref[...])
    o_ref[...] = _rms_scale(x, w_ref[...], EPS).astype(o_ref.dtype)


def _rmsnorm(xp, xs, w, out_dtype, tm):
    (tp, d), ms = xp.shape, xs.shape[0]
    ntp = tp // tm
    return pl.pallas_call(
        functools.partial(_rmsnorm_kernel, ntp=ntp),
        grid=((tp + ms) // tm,),
        in_specs=[pl.BlockSpec((tm, d), lambda i: (_prompt_rows(i, ntp), 0)),
                  pl.BlockSpec((tm, d), lambda i: (_sample_rows(i, ntp), 0)),
                  pl.BlockSpec((1, d), lambda i: (0, 0))],
        out_specs=pl.BlockSpec((tm, d), lambda i: (i, 0)),
        out_shape=jax.ShapeDtypeStruct((tp + ms, d), out_dtype),
        compiler_params=_params("arbitrary"),
        name="rmsnorm",
    )(xp, xs, w.reshape(1, d))


def _rope_heads(z, cos, sin_signed):
    tm = z.shape[0]
    lane = lax.broadcasted_iota(jnp.int32, (tm, HEAD_DIM), 1)
    first_half = (lane % DK_DIFF) < (DK_DIFF // 2)
    outs = []
    for h in range(z.shape[1] // HEAD_DIM):
        zh = z[:, h * HEAD_DIM:(h + 1) * HEAD_DIM]
        partner = jnp.where(first_half, pltpu.roll(zh, HEAD_DIM - DK_DIFF // 2, 1), pltpu.roll(zh, DK_DIFF // 2, 1))
        outs.append(zh * cos + partner * sin_signed)
    return jnp.concatenate(outs, axis=1)


def _proj_kernel(*refs, epilogue, scale, ntp, aliased):
    h_ref, w_ref, w16_scr = refs[0], refs[1], refs[-1]
    refs = refs[2:-1]
    j, i = pl.program_id(0), pl.program_id(1)

    @pl.when(i == 0)
    def _():
        w16_scr[...] = w_ref[...].astype(_BF16)

    z = jnp.dot(h_ref[...], w16_scr[...], preferred_element_type=_F32)

    def store_kv(val, op_ref, os_ref, o16_ref):
        o16_ref[...] = val.astype(_BF16)

        @pl.when(i < ntp)
        def _():
            op_ref[...] = val

        @pl.when(i >= ntp)
        def _():
            os_ref[...] = val

    if epilogue == "plain":
        refs[0][...] = z
    elif epilogue == "silu":
        refs[0][...] = z * jax.nn.sigmoid(z)
    elif epilogue == "scale_bf16":
        refs[0][...] = (z * scale).astype(_BF16)
    elif epilogue == "rope_scale_bf16":
        cos_ref, sin_ref, o_ref = refs
        o_ref[...] = (_rope_heads(z, cos_ref[...], sin_ref[...]) * scale).astype(_BF16)
    elif epilogue == "key":
        cos_ref, sin_ref = refs[:2]
        outs = refs[3:] if aliased else refs[2:]

        @pl.when(j == 0)
        def _():
            store_kv(_rope_heads(z, cos_ref[...], sin_ref[...]), *outs)

        @pl.when(j != 0)
        def _():
            store_kv(z, *outs)
    elif epilogue == "value":
        store_kv(z, *(refs[1:] if aliased else refs))
    elif epilogue == "gate":
        b_ref, o_ref = refs
        o_ref[...] = jax.nn.sigmoid(z + b_ref[...])
    else:
        raise ValueError(epilogue)


def _proj(h, w_all, layer, col_off, n_cols, epilogue, *, tm, tn, tp, extras=(), scale=1.0, stack=None):
    m, d = h.shape
    depth = w_all.shape[0]
    assert col_off % tn == 0 and n_cols % tn == 0 and tp % tm == 0
    off_blk = col_off // tn
    ntp = tp // tm
    grid = (n_cols // tn, m // tm)
    in_specs = [pl.BlockSpec((tm, d), lambda j, i: (i, 0)),
                pl.BlockSpec((None, d, tn), lambda j, i: (layer, 0, off_blk + j))]
    row_tab = pl.BlockSpec((tm, HEAD_DIM), lambda j, i: (i, 0))
    out_tile = pl.BlockSpec((tm, tn), lambda j, i: (i, j))
    f32_out = jax.ShapeDtypeStruct((m, n_cols), _F32)
    b16_out = jax.ShapeDtypeStruct((m, n_cols), _BF16)
    kv_specs = [pl.BlockSpec((None, tm, tn), lambda j, i: (layer, _prompt_rows(i, ntp), j)),
                pl.BlockSpec((tm, tn), lambda j, i: (_sample_rows(i, ntp), j)), out_tile]
    kv_shapes = [jax.ShapeDtypeStruct((depth, tp, n_cols), _F32), jax.ShapeDtypeStruct((m - tp, n_cols), _F32), b16_out]
    aliases = {}
    if epilogue in ("plain", "silu"):
        out_specs, out_shape = out_tile, f32_out
    elif epilogue == "scale_bf16":
        out_specs, out_shape = out_tile, b16_out
    elif epilogue == "rope_scale_bf16":
        in_specs += [row_tab, row_tab]
        out_specs, out_shape = out_tile, b16_out
    elif epilogue in ("key", "value"):
        if epilogue == "key":
            assert n_cols == 2 * tn
            in_specs += [row_tab, row_tab]
        out_specs, out_shape = kv_specs, kv_shapes
        if stack is not None:
            in_specs += [pl.BlockSpec(memory_space=pl.ANY)]
            extras = tuple(extras) + (stack,)
            aliases = {len(in_specs) - 1: 0}
    elif epilogue == "gate":
        in_specs += [pl.BlockSpec((1, tn), lambda j, i: (0, j))]
        out_specs, out_shape = out_tile, f32_out
    else:
        raise ValueError(epilogue)
    return pl.pallas_call(
        functools.partial(_proj_kernel, epilogue=epilogue, scale=scale, ntp=ntp, aliased=bool(aliases)),
        grid=grid, in_specs=in_specs, out_specs=out_specs, out_shape=out_shape,
        scratch_shapes=[pltpu.VMEM((d, tn), _BF16)],
        input_output_aliases=aliases,
        compiler_params=_params("arbitrary", "arbitrary"),
        name="proj_" + epilogue,
    )(h, w_all, *extras)


def _pool_prompt_kernel(u_ref, prev_ref, d_ref, *, tm, pg):
    i = pl.program_id(0)
    u = u_ref[...]
    prev = jnp.where(i > 0, prev_ref[...], 0.0)
    ext = jnp.concatenate([prev, u], axis=0)
    t = i * tm + lax.broadcasted_iota(jnp.int32, (tm, 1), 0)
    for g, w in enumerate(POOL_WINDOWS):
        s = ext[:, g * pg:(g + 1) * pg]
        sh = 1
        while sh < w:
            s = s + pltpu.roll(s, sh, 0)
            sh *= 2
        cnt = jnp.minimum(w, t + 1).astype(_F32)
        d_ref[:, g * pg:(g + 1) * pg] = s[POOL_HALO:] / cnt - u[:, g * pg:(g + 1) * pg]


def _pool_prompt(u, tp, tm):
    m, p = u.shape
    pg = p // len(POOL_WINDOWS)
    hb = tm // POOL_HALO
    return pl.pallas_call(
        functools.partial(_pool_prompt_kernel, tm=tm, pg=pg),
        grid=(tp // tm,),
        in_specs=[pl.BlockSpec((tm, p), lambda i: (i, 0)),
                  pl.BlockSpec((POOL_HALO, p), lambda i: (jnp.maximum(i * hb - 1, 0), 0))],
        out_specs=pl.BlockSpec((tm, p), lambda i: (i, 0)),
        out_shape=jax.ShapeDtypeStruct((m, p), _F32),
        compiler_params=_params("parallel"),
        name="pool_prompt",
    )(u, u)


def _pool_sample_kernel(st_ref, un_ref, d_ref, *, pg):
    nb, ts = st_ref.shape[0], un_ref.shape[0]
    for g, w in enumerate(POOL_WINDOWS):
        cols = slice(g * pg, (g + 1) * pg)
        rows = [st_ref[r, :, cols] for r in range(nb)] + [un_ref[t, :, cols] for t in range(ts)]
        for t in range(ts):
            r = nb + t
            lo = max(r + 1 - w, 0)
            s = rows[lo]
            for rr in range(lo + 1, r + 1):
                s = s + rows[rr]
            d_ref[t, :, cols] = s / float(r + 1 - lo) - rows[r]


def _pool_sample(state_t, u_new_t):
    nb, bs, p = state_t.shape
    ts = u_new_t.shape[0]
    pg = p // len(POOL_WINDOWS)
    return pl.pallas_call(
        functools.partial(_pool_sample_kernel, pg=pg),
        grid=(1,),
        in_specs=[pl.BlockSpec((nb, bs, p), lambda i: (0, 0, 0)), pl.BlockSpec((ts, bs, p), lambda i: (0, 0, 0))],
        out_specs=pl.BlockSpec((ts, bs, p), lambda i: (0, 0, 0)),
        out_shape=jax.ShapeDtypeStruct((ts, bs, p), _F32),
        compiler_params=_params("arbitrary"),
        name="pool_sample",
    )(state_t, u_new_t)


def _lambda_full(lam_ref, lam0):
    dl = lam_ref[...]
    a = jnp.sum(dl[0:1] * dl[1:2], axis=-1, keepdims=True)
    b = jnp.sum(dl[2:3] * dl[3:4], axis=-1, keepdims=True)
    return jnp.exp(a) - jnp.exp(b) + lam0


def _dot_nt(a, b):
    return lax.dot_general(a, b, (((1,), (1,)), ((), ())), preferred_element_type=_F32)


def _softmax_init(m_scr, l_scr, acc_scr):
    m_scr[...] = jnp.full(m_scr.shape, NEG_BIG, _F32)
    l_scr[...] = jnp.zeros(l_scr.shape, _F32)
    acc_scr[...] = jnp.zeros(acc_scr.shape, _F32)


def _softmax_update(s_parts, v_parts, m_scr, l_scr, acc_scr, valid=None):
    part_chunks = []
    for s in s_parts:
        chunks = [s[:, c * HEAD_DIM:(c + 1) * HEAD_DIM] for c in range(s.shape[1] // HEAD_DIM)]
        if valid is not None:
            chunks = [jnp.where(valid, c, NEG_BIG) for c in chunks]
        part_chunks.append(chunks)
    flat = [c for chunks in part_chunks for c in chunks]
    mx = flat[0]
    for c in flat[1:]:
        mx = jnp.maximum(mx, c)
    m_old = m_scr[...]
    m_new = jnp.maximum(m_old, jnp.max(mx, axis=-1, keepdims=True))
    alpha = jnp.exp2(m_old - m_new)
    acc = alpha * acc_scr[...]
    lsum = alpha * l_scr[...]
    for chunks, v16 in zip(part_chunks, v_parts):
        ps = [jnp.exp2(c - m_new) for c in chunks]
        for p in ps:
            lsum = lsum + p
        p16 = jnp.concatenate([p.astype(_BF16) for p in ps], axis=1)
        acc = acc + jnp.dot(p16, v16, preferred_element_type=_F32)
    l_scr[...] = lsum
    acc_scr[...] = acc
    m_scr[...] = m_new


def _softmax_result(l_scr, acc_scr):
    return acc_scr[...] / jnp.sum(l_scr[...], axis=-1, keepdims=True)


def _diff_prompt_kernel(lam_ref, subw_ref, q_ref, k_ref, v_ref, o_ref, m_scr, l_scr, acc_scr, *, lam0, tq):
    i = pl.program_id(1)
    q = q_ref[...]
    lane = lax.broadcasted_iota(jnp.int32, q.shape, 1)
    zero = jnp.zeros_like(q)
    qq = jnp.concatenate([jnp.where(lane < DK_DIFF, q, zero), jnp.where(lane >= DK_DIFF, q, zero)], axis=0)
    _softmax_init(m_scr, l_scr, acc_scr)

    def block(j, masked):
        start = pl.multiple_of(j * tq, tq)
        s = _dot_nt(qq, k_ref[pl.ds(start, tq), :])
        if masked:
            row = lax.broadcasted_iota(jnp.int32, s.shape, 0) % tq
            col = lax.broadcasted_iota(jnp.int32, s.shape, 1)
            s = jnp.where(col <= row, s, NEG_BIG)
        _softmax_update([s], [v_ref[pl.ds(start, tq), :]], m_scr, l_scr, acc_scr)

    def body(j, carry):
        block(j, False)
        return carry

    lax.fori_loop(0, i, body, 0)
    block(i, True)

    o = _softmax_result(l_scr, acc_scr)
    lam = _lambda_full(lam_ref, lam0)
    od = o[:tq] - lam * o[tq:]
    ms = jnp.mean(od * od, axis=-1, keepdims=True)
    o_ref[...] = od * lax.rsqrt(ms + SUBLN_EPS) * subw_ref[...] * (1.0 - lam0)


def _diff_prompt(lam_p, subw, q_d, k16, v16, tp, n_heads, m_rows, lam0, tq):
    return pl.pallas_call(
        functools.partial(_diff_prompt_kernel, lam0=lam0, tq=tq),
        grid=(n_heads, tp // tq),
        in_specs=[pl.BlockSpec(lam_p.shape, lambda h, i: (0, 0)),
                  pl.BlockSpec((1, HEAD_DIM), lambda h, i: (0, 0)),
                  pl.BlockSpec((tq, HEAD_DIM), lambda h, i: (i, h)),
                  pl.BlockSpec((tp, HEAD_DIM), lambda h, i: (0, h)),
                  pl.BlockSpec((tp, HEAD_DIM), lambda h, i: (0, h))],
        out_specs=pl.BlockSpec((tq, HEAD_DIM), lambda h, i: (i, h)),
        out_shape=jax.ShapeDtypeStruct((m_rows, n_heads * HEAD_DIM), _F32),
        scratch_shapes=[pltpu.VMEM((2 * tq, HEAD_DIM), _F32)] * 3,
        compiler_params=_params("parallel", "arbitrary"),
        name="diff_prompt",
    )(lam_p, subw, q_d, k16, v16)


def _log_sigmoid_pair(s):
    t = jnp.log1p(jnp.exp(-jnp.abs(s)))
    return jnp.minimum(s, 0.0) - t, -jnp.maximum(s, 0.0) - t


def _suffix_sums(lr, tri):
    hi = lr.astype(_BF16)
    lo = (lr - hi.astype(_F32)).astype(_BF16)
    return (jnp.dot(hi, tri, preferred_element_type=_F32) + jnp.dot(lo, tri, preferred_element_type=_F32))


def _strict_lower_ones(n):
    r = lax.broadcasted_iota(jnp.int32, (n, n), 0)
    c = lax.broadcasted_iota(jnp.int32, (n, n), 1)
    return (r > c).astype(_BF16)


def _sb_prompt_kernel(q_ref, k_ref, v_ref, o_ref, c_scr, acc_scr, *, tq, nh):
    i = pl.program_id(1)
    tri = _strict_lower_ones(tq)
    c_scr[...] = jnp.zeros(c_scr.shape, _F32)
    acc_scr[...] = jnp.zeros(acc_scr.shape, _F32)

    def block(j, masked):
        start = pl.multiple_of(j * tq, tq)
        for hh in range(nh):
            cols = slice(hh * HEAD_DIM, (hh + 1) * HEAD_DIM)
            s = _dot_nt(q_ref[:, cols], k_ref[pl.ds(start, tq), cols])
            lb, lr = _log_sigmoid_pair(s)
            if masked:
                row = lax.broadcasted_iota(jnp.int32, s.shape, 0)
                col = lax.broadcasted_iota(jnp.int32, s.shape, 1)
                valid = col < row
                lr = jnp.where(valid, lr, 0.0)
            c = c_scr[hh]
            a = jnp.exp(lb + _suffix_sums(lr, tri) + c)
            if masked:
                a = jnp.where(valid, a, 0.0)
            acc_scr[hh] += jnp.dot(a.astype(_BF16), v_ref[pl.ds(start, tq), cols], preferred_element_type=_F32)
            c_scr[hh] = c + jnp.sum(lr, axis=-1, keepdims=True)

    block(i, True)

    def cond(j):
        return jnp.logical_and(j >= 0, jnp.max(c_scr[...]) > SB_CUTOFF)

    def body(j):
        block(j, False)
        return j - 1

    lax.while_loop(cond, body, i - 1)
    for hh in range(nh):
        o_ref[:, hh * HEAD_DIM:(hh + 1) * HEAD_DIM] = acc_scr[hh]


def _sb_prompt(q_s, k16, v16, tp, n_heads, head_off, m_rows, tq):
    nh = SB_HEADS_PER_STEP
    assert n_heads % nh == 0 and head_off % nh == 0
    w = nh * HEAD_DIM
    return pl.pallas_call(
        functools.partial(_sb_prompt_kernel, tq=tq, nh=nh),
        grid=(n_heads // nh, tp // tq),
        in_specs=[pl.BlockSpec((tq, w), lambda h, i: (i, h)),
                  pl.BlockSpec((tp, w), lambda h, i: (0, head_off // nh + h)),
                  pl.BlockSpec((tp, w), lambda h, i: (0, head_off // nh + h))],
        out_specs=pl.BlockSpec((tq, w), lambda h, i: (i, h)),
        out_shape=jax.ShapeDtypeStruct((m_rows, n_heads * HEAD_DIM), _F32),
        scratch_shapes=[pltpu.VMEM((nh, tq, 1), _F32), pltpu.VMEM((nh, tq, HEAD_DIM), _F32)],
        compiler_params=_params("parallel", "arbitrary"),
        name="sb_prompt",
    )(q_s, k16, v16)


def _own_lanes(rows, rows_per_head, hg):
    lane = lax.broadcasted_iota(jnp.int32, (rows, HEAD_DIM), 1)
    row = lax.broadcasted_iota(jnp.int32, (rows, HEAD_DIM), 0)
    return lane, row, (lane % hg) == (row // rows_per_head)


def _group_rows(ref, hg):
    x = ref[...]
    return x.reshape(x.shape[0] * hg, HEAD_DIM).astype(_BF16)


def _sample_diff_kernel(pt_ref, lam_ref, subw_ref, qd_ref, kn_ref, vn_ref, *rest, lam0, ts, hg, gp):
    del pt_ref
    k_refs, v_refs = rest[:gp], rest[gp:2 * gp]
    od_ref, m_scr, l_scr, acc_scr = rest[2 * gp:]
    p = pl.program_id(1)
    qd = qd_ref[...]
    rd = qd.shape[0]
    lane, row, own = _own_lanes(rd, 2 * ts, hg)

    def block(k_refs, v_refs, valid):
        s_parts = [_dot_nt(qd, _group_rows(k_ref, hg)) for k_ref in k_refs]
        _softmax_update(s_parts, [_group_rows(v_ref, hg) for v_ref in v_refs], m_scr, l_scr, acc_scr, valid=valid)

    @pl.when(p == 0)
    def _():
        _softmax_init(m_scr, l_scr, acc_scr)
        block([kn_ref], [vn_ref], jnp.logical_and(own, jnp.logical_and(lane // hg < ts, lane // hg <= row % ts)))

    block(k_refs, v_refs, own)

    @pl.when(p == pl.num_programs(1) - 1)
    def _():
        o = _softmax_result(l_scr, acc_scr)
        lam = _lambda_full(lam_ref, lam0)
        od = o - lam * pltpu.roll(o, rd - ts, 0)
        ms = jnp.mean(od * od, axis=-1, keepdims=True)
        od_ref[...] = od * lax.rsqrt(ms + SUBLN_EPS) * subw_ref[...] * (1.0 - lam0)


def _sb_block(qs, k_ref, v_ref, valid, tmat_ref, run_scr, acc_scr, hg):
    rs = qs.shape[0]
    s = _dot_nt(qs, _group_rows(k_ref, hg))
    nch = s.shape[1] // HEAD_DIM
    lbs, lrs = [], []
    for c in range(nch):
        lb, lr = _log_sigmoid_pair(s[:, c * HEAD_DIM:(c + 1) * HEAD_DIM])
        lbs.append(lb)
        lrs.append(jnp.where(valid, lr, 0.0))
    lr_all = jnp.concatenate(lrs, axis=0)
    hi = lr_all.astype(_BF16)
    lo = (lr_all - hi.astype(_F32)).astype(_BF16)
    tm = tmat_ref[...]
    r = jnp.dot(hi, tm, preferred_element_type=_F32) + jnp.dot(lo, tm, preferred_element_type=_F32)
    run = run_scr[...]
    a_parts = [None] * nch
    for c in reversed(range(nch)):
        rc = r[c * rs:(c + 1) * rs]
        a = jnp.exp(lbs[c] + rc[:, :HEAD_DIM] + run)
        a_parts[c] = jnp.where(valid, a, 0.0).astype(_BF16)
        run = run + rc[:, HEAD_DIM:]
    run_scr[...] = run
    acc_scr[...] += jnp.dot(jnp.concatenate(a_parts, axis=1), _group_rows(v_ref, hg), preferred_element_type=_F32)


def _sb_pages(qs, k_refs, v_refs, own, tmat_ref, run_scr, acc_scr, hg):
    def go(g):
        @pl.when(jnp.max(jnp.where(own, run_scr[...], NEG_BIG)) > SB_CUTOFF)
        def _():
            _sb_block(qs, k_refs[g], v_refs[g], own, tmat_ref, run_scr, acc_scr, hg)
            if g + 1 < len(k_refs):
                go(g + 1)

    go(0)


def _sample_sb_head_kernel(pt_ref, tmat_ref, qs_ref, kn_ref, vn_ref, *rest, ts, hg, gp):
    del pt_ref
    k_refs, v_refs = rest[:gp], rest[gp:2 * gp]
    acc_ref, run_ref, run_scr, acc_scr = rest[2 * gp:]
    qs = qs_ref[...]
    lane, row, own = _own_lanes(qs.shape[0], ts, hg)
    run_scr[...] = jnp.zeros(run_scr.shape, _F32)
    acc_scr[...] = jnp.zeros(acc_scr.shape, _F32)
    _sb_block(qs, kn_ref, vn_ref, jnp.logical_and(own, jnp.logical_and(lane // hg < ts, lane // hg < row % ts)),
              tmat_ref, run_scr, acc_scr, hg)
    _sb_pages(qs, k_refs, v_refs, own, tmat_ref, run_scr, acc_scr, hg)
    acc_ref[...] = acc_scr[...]
    run_ref[...] = run_scr[...]


def _sample_sb_tail_kernel(pt_ref, live_ref, tmat_ref, qs_ref, acc_in_ref, run_in_ref, *rest, ts, hg, gp):
    del pt_ref
    k_refs, v_refs = rest[:gp], rest[gp:2 * gp]
    acc_ref, run_scr, acc_scr = rest[2 * gp:]
    b, p = pl.program_id(0), pl.program_id(1)
    qs = qs_ref[...]
    _, _, own = _own_lanes(qs.shape[0], ts, hg)

    @pl.when(p == 0)
    def _():
        run_scr[...] = run_in_ref[...]
        acc_scr[...] = acc_in_ref[...]

    @pl.when(live_ref[b] != 0)
    def _():
        _sb_pages(qs, k_refs, v_refs, own, tmat_ref, run_scr, acc_scr, hg)

    @pl.when(p == pl.num_programs(1) - 1)
    def _():
        acc_ref[...] = acc_scr[...]


def _suffix_matrix(hg):
    idx = jnp.arange(HEAD_DIM)
    tok, head = idx // hg, idx % hg
    same = head[:, None] == head[None, :]
    later = jnp.logical_and(same, tok[:, None] > tok[None, :])
    return jnp.concatenate([later, same], axis=1).astype(_BF16)


def _largest_divisor(n, cap):
    return max(d for d in range(1, cap + 1) if n % d == 0)


def _sample_attn(page_table, lam_p, subw, qd_rows, qs_rows, kn, vn, cache_k, cache_v, layer, lam0, ts):
    bs, n_pages = page_table.shape
    page, n_heads = cache_k.shape[2], cache_k.shape[3]
    hg = n_heads // 2
    rd, rs = qd_rows.shape[1], qs_rows.shape[1]
    new_tok = kn.shape[1]
    assert hg == 8 and new_tok * hg == HEAD_DIM and (page * hg) % HEAD_DIM == 0
    pt_flat = page_table.reshape(-1)
    tmat = _suffix_matrix(hg)
    half = (None, None, page, hg, HEAD_DIM)
    state = jax.ShapeDtypeStruct((bs, rs, HEAD_DIM), _F32)

    def per_b(shape):
        return pl.BlockSpec((None,) + shape, lambda b, *_: (b,) + (0,) * len(shape))

    def const(shape):
        return pl.BlockSpec(shape, lambda *_: (0,) * len(shape))

    def new_keys(group):
        return pl.BlockSpec((None, new_tok, hg, HEAD_DIM), lambda b, *_: (b, 0, group, 0))

    gp = _pow2_tile(PAGES_PER_STEP, n_pages)

    def diff_page(g):
        return pl.BlockSpec(half, lambda b, p, pt: (layer, pt[b * n_pages + (n_pages - 1 - (p * gp + g))], 0, 0, 0))

    od = pl.pallas_call(
        functools.partial(_sample_diff_kernel, lam0=lam0, ts=ts, hg=hg, gp=gp),
        grid_spec=pltpu.PrefetchScalarGridSpec(
            num_scalar_prefetch=1, grid=(bs, n_pages // gp),
            in_specs=[const(lam_p.shape), const((1, HEAD_DIM)), per_b((rd, HEAD_DIM)), new_keys(0), new_keys(0)]
            + [diff_page(g) for g in range(gp)] + [diff_page(g) for g in range(gp)],
            out_specs=per_b((rd, HEAD_DIM)),
            scratch_shapes=[pltpu.VMEM((rd, HEAD_DIM), _F32)] * 3),
        out_shape=jax.ShapeDtypeStruct((bs, rd, HEAD_DIM), _F32),
        compiler_params=_params("parallel", "arbitrary"),
        name="sample_diff",
    )(pt_flat, lam_p, subw, qd_rows, kn, vn, *([cache_k] * gp), *([cache_v] * gp))

    g1 = min(SB_HEAD_PAGES, n_pages)

    def head_page(g):
        return pl.BlockSpec(half, lambda b, pt: (layer, pt[b * n_pages + (n_pages - 1 - g)], 0, 1, 0))

    acc, run = pl.pallas_call(
        functools.partial(_sample_sb_head_kernel, ts=ts, hg=hg, gp=g1),
        grid_spec=pltpu.PrefetchScalarGridSpec(
            num_scalar_prefetch=1, grid=(bs,),
            in_specs=[const(tmat.shape), per_b((rs, HEAD_DIM)), new_keys(1), new_keys(1)]
            + [head_page(g) for g in range(g1)] + [head_page(g) for g in range(g1)],
            out_specs=[per_b((rs, HEAD_DIM)), per_b((rs, HEAD_DIM))],
            scratch_shapes=[pltpu.VMEM((rs, HEAD_DIM), _F32)] * 2),
        out_shape=[state, state],
        compiler_params=_params("parallel"),
        name="sample_sb_head",
    )(pt_flat, tmat, qs_rows, kn, vn, *([cache_k] * g1), *([cache_v] * g1))

    n_rest = n_pages - g1
    if n_rest:
        lane = jnp.arange(HEAD_DIM)[None, :] % hg
        head = jnp.arange(rs)[:, None] // ts
        live = (jnp.max(jnp.where(lane == head, run, NEG_BIG), axis=(1, 2)) > SB_CUTOFF).astype(jnp.int32)
        gt = _largest_divisor(n_rest, SB_TAIL_PAGES_PER_STEP)

        def tail_page(g):
            def index(b, p, pt, lv):
                phys = pt[b * n_pages + (n_pages - 1 - (g1 + p * gt + g))]
                return (layer, jnp.where(lv[b] != 0, phys, 0), 0, 1, 0)
            return pl.BlockSpec(half, index)

        def tail(acc_in):
            return pl.pallas_call(
                functools.partial(_sample_sb_tail_kernel, ts=ts, hg=hg, gp=gt),
                grid_spec=pltpu.PrefetchScalarGridSpec(
                    num_scalar_prefetch=2, grid=(bs, n_rest // gt),
                    in_specs=[const(tmat.shape), per_b((rs, HEAD_DIM)), per_b((rs, HEAD_DIM)), per_b((rs, HEAD_DIM))]
                    + [tail_page(g) for g in range(gt)] + [tail_page(g) for g in range(gt)],
                    out_specs=per_b((rs, HEAD_DIM)),
                    scratch_shapes=[pltpu.VMEM((rs, HEAD_DIM), _F32)] * 2),
                out_shape=state,
                compiler_params=_params("parallel", "arbitrary"),
                name="sample_sb_tail",
            )(pt_flat, live, tmat, qs_rows, acc_in, run, *([cache_k] * gt), *([cache_v] * gt))

        acc = lax.cond(jnp.any(live != 0), tail, lambda acc_in: acc_in, acc)
    return od, acc


def _sample_query_rows(q_d, q_s, bs, ts, n_diff, n_sb):
    qd = q_d.reshape(bs, ts, n_diff, 2, DK_DIFF)
    z = jnp.zeros_like(qd[..., 0, :])
    comps = jnp.stack([jnp.concatenate([qd[..., 0, :], z], axis=-1),
                       jnp.concatenate([z, qd[..., 1, :]], axis=-1)], axis=3)
    qd_rows = jnp.transpose(comps, (0, 2, 3, 1, 4)).reshape(bs, n_diff * 2 * ts, HEAD_DIM)
    qs_rows = jnp.transpose(q_s.reshape(bs, ts, n_sb, HEAD_DIM), (0, 2, 1, 3)).reshape(bs, n_sb * ts, HEAD_DIM)
    return qd_rows, qs_rows


def _branch_kernel(d_ref, sgp_ref, od_ref, os_ref, sga_ref, gate_ref, pw_ref, ps_ref, wb_ref, o_ref, *, pg):
    p = d_ref.shape[1]
    dm = o_ref.shape[1]
    d = d_ref[...]
    ys = [jnp.dot(d[:, g * pg:(g + 1) * pg].astype(_BF16), pw_ref[g], preferred_element_type=_F32)
          for g in range(len(POOL_WINDOWS))]
    y_pool = jnp.concatenate(ys, axis=1) * ps_ref[...]
    b0 = (y_pool * sgp_ref[...]).astype(_BF16)
    b1 = (od_ref[...] * sga_ref[:, :p]).astype(_BF16)
    b2 = (os_ref[...] * sga_ref[:, p:]).astype(_BF16)
    merged = gate_ref[:, 0:dm] * jnp.dot(b0, wb_ref[0], preferred_element_type=_F32)
    merged += gate_ref[:, dm:2 * dm] * jnp.dot(b1, wb_ref[1], preferred_element_type=_F32)
    merged += gate_ref[:, 2 * dm:3 * dm] * jnp.dot(b2, wb_ref[2], preferred_element_type=_F32)
    o_ref[...] = merged.astype(_BF16)


def _branch_merge(d, sg_pool, od, os_, sg_att, gates, pool_w16, pool_scale, w_branch16, tm):
    m, p = d.shape
    dm = w_branch16.shape[2]
    pg = p // len(POOL_WINDOWS)
    row = lambda w: pl.BlockSpec((tm, w), lambda i: (i, 0))
    const = lambda shape: pl.BlockSpec(shape, lambda i: (0,) * len(shape), pipeline_mode=pl.Buffered(1))
    return pl.pallas_call(
        functools.partial(_branch_kernel, pg=pg),
        grid=(m // tm,),
        in_specs=[row(p), row(p), row(p), row(p), row(2 * p), row(3 * dm),
                  const(pool_w16.shape), const((1, p)), const(w_branch16.shape)],
        out_specs=row(dm),
        out_shape=jax.ShapeDtypeStruct((m, dm), _BF16),
        compiler_params=_params("parallel"),
        name="branch_merge",
    )(d, sg_pool, od, os_, sg_att, gates, pool_w16, pool_scale.reshape(1, p), w_branch16)


def _out_kernel(*refs, x_pair, final, ntp):
    i = pl.program_id(0)
    mg_ref, wo_ref = refs[:2]
    if x_pair:
        x_in = jnp.where(i < ntp, refs[2][...], refs[3][...])
        refs = refs[4:]
    else:
        x_in = refs[2][...]
        refs = refs[3:]
    wn_ref = refs[0]
    x = x_in + jnp.dot(mg_ref[...], wo_ref[...], preferred_element_type=_F32)
    hn = _rms_scale(x, wn_ref[...], EPS)
    if final:
        yp_ref, ys_ref = refs[1:]

        @pl.when(i < ntp)
        def _():
            yp_ref[...] = hn

        @pl.when(i >= ntp)
        def _():
            ys_ref[...] = hn
    else:
        xo_ref, hn_ref = refs[1:]
        xo_ref[...] = x
        hn_ref[...] = hn.astype(hn_ref.dtype)


def _out_proj(merged, w_out16, x, w_next, tp, tm, final):
    m, dm = merged.shape
    ntp = tp // tm
    x_pair = isinstance(x, tuple)
    row = pl.BlockSpec((tm, dm), lambda i: (i, 0))
    p_rows = pl.BlockSpec((tm, dm), lambda i: (_prompt_rows(i, ntp), 0))
    s_rows = pl.BlockSpec((tm, dm), lambda i: (_sample_rows(i, ntp), 0))
    in_specs = [row, pl.BlockSpec((dm, dm), lambda i: (0, 0), pipeline_mode=pl.Buffered(1))]
    in_specs += [p_rows, s_rows] if x_pair else [row]
    in_specs += [pl.BlockSpec((1, dm), lambda i: (0, 0))]
    if final:
        out_specs = [p_rows, s_rows]
        out_shape = [jax.ShapeDtypeStruct((tp, dm), _F32), jax.ShapeDtypeStruct((m - tp, dm), _F32)]
    else:
        out_specs = [row, row]
        out_shape = [jax.ShapeDtypeStruct((m, dm), _F32), jax.ShapeDtypeStruct((m, dm), _BF16)]
    return pl.pallas_call(
        functools.partial(_out_kernel, x_pair=x_pair, final=final, ntp=ntp),
        grid=(m // tm,),
        in_specs=in_specs, out_specs=out_specs, out_shape=out_shape,
        compiler_params=_params("arbitrary"),
        name="out_proj",
    )(merged, w_out16, *(x if x_pair else (x,)), w_next.reshape(1, dm))


def _rope_tables(pos):
    half = DK_DIFF // 2
    inv = 1.0 / (ROPE_THETA ** (jnp.arange(half, dtype=_F32) * 2.0 / DK_DIFF))
    ang = pos.astype(_F32)[:, None] * inv[None, :]
    cos, sin = jnp.cos(ang), jnp.sin(ang)
    reps = HEAD_DIM // DK_DIFF
    return jnp.tile(jnp.concatenate([cos, cos], axis=1), (1, reps)), jnp.tile(jnp.concatenate([-sin, sin], axis=1), (1, reps))


def kernel(x_prompt, x_sample, cache_k, cache_v, state_pool, page_table, norm_w, w_in, gate_b, pool_w, pool_scale,
           diff_lambda, diff_subln_w, w_branch, w_out, norm_f_w):
    bp, tp, dm = x_prompt.shape
    bs, ts, _ = x_sample.shape
    assert bp == 1
    depth = w_in.shape[0]
    n_pages = page_table.shape[1]
    page, n_heads = cache_k.shape[2], cache_k.shape[3]
    assert cache_k.shape[4] == HEAD_DIM and state_pool.shape[2] == POOL_BUF
    ms = bs * ts
    m = tp + ms
    bw = dm // 2
    n_diff = bw // HEAD_DIM
    n_sb = bw // HEAD_DIM
    assert n_heads == n_diff + n_sb
    kvw = n_heads * HEAD_DIM
    offs = {}
    o = 0
    for name, size in (("u", bw), ("g_pool", bw), ("q_d", bw), ("q_s", bw), ("k", kvw), ("v", kvw),
                       ("g_att", 2 * bw), ("g_br", 3 * dm)):
        offs[name] = o
        o += size
    assert o == w_in.shape[2]

    tm = _pow2_tile(512, tp, ms)
    tn = _pow2_tile(1024, bw)
    tq = _pow2_tile(DIFF_Q_TILE, tp)
    tqs = _pow2_tile(SB_Q_TILE, tp)
    tmb = _pow2_tile(256, tp, ms)

    pos = jnp.concatenate([jnp.arange(tp, dtype=jnp.int32),
                           jnp.tile(n_pages * page + jnp.arange(ts, dtype=jnp.int32), bs)])
    cos_t, sin_t = _rope_tables(pos)

    x = (x_prompt.reshape(tp, dm), x_sample.reshape(ms, dm))
    h = _rmsnorm(*x, norm_w[0], _BF16, tm)

    kp_stack = vp_stack = y_p = y_s = None
    pool_p, k_s, v_s, pool_s = [], [], [], []
    for l in range(depth):
        lam0 = _lambda_init(l)
        proj = functools.partial(_proj, h, w_in, l, tm=tm, tn=tn, tp=tp)
        u = proj(offs["u"], bw, "plain")
        sg_pool = proj(offs["g_pool"], bw, "silu")
        q_d = proj(offs["q_d"], bw, "rope_scale_bf16", extras=(cos_t, sin_t), scale=DK_DIFF ** -0.5 * LOG2E)
        q_s = proj(offs["q_s"], bw, "scale_bf16", scale=HEAD_DIM ** -0.5)
        kp_stack, ks32, k16 = proj(offs["k"], kvw, "key", extras=(cos_t, sin_t), stack=kp_stack)
        vp_stack, vs32, v16 = proj(offs["v"], kvw, "value", stack=vp_stack)
        sg_att = proj(offs["g_att"], 2 * bw, "silu")
        gates = proj(offs["g_br"], 3 * dm, "gate", extras=(gate_b[l].reshape(1, 3 * dm),))

        d = _pool_prompt(u, tp, tm)
        u_s = u[tp:].reshape(bs, ts, bw)
        d_s = _pool_sample(jnp.transpose(state_pool[l], (1, 0, 2)), jnp.transpose(u_s, (1, 0, 2)))
        d = lax.dynamic_update_slice(d, jnp.transpose(d_s, (1, 0, 2)).reshape(ms, bw), (tp, 0))

        lam_p = diff_lambda[l]
        subw = diff_subln_w[l].reshape(1, HEAD_DIM)
        od = _diff_prompt(lam_p, subw, q_d, k16, v16, tp, n_diff, m, lam0, tq)
        os_ = _sb_prompt(q_s, k16, v16, tp, n_sb, n_diff, m, tqs)
        qd_rows, qs_rows = _sample_query_rows(q_d[tp:], q_s[tp:], bs, ts, n_diff, n_sb)
        new_tok = HEAD_DIM // (n_heads // 2)
        pad = ((0, 0), (0, new_tok - ts), (0, 0), (0, 0))
        kn = jnp.pad(ks32.reshape(bs, ts, n_heads, HEAD_DIM), pad)
        vn = jnp.pad(vs32.reshape(bs, ts, n_heads, HEAD_DIM), pad)
        od_s, os_s = _sample_attn(page_table, lam_p, subw, qd_rows, qs_rows, kn, vn, cache_k, cache_v, l, lam0, ts)
        od_s = jnp.transpose(od_s.reshape(bs, n_diff, 2 * ts, HEAD_DIM)[:, :, :ts], (0, 2, 1, 3))
        os_s = jnp.transpose(os_s.reshape(bs, n_sb, ts, HEAD_DIM), (0, 2, 1, 3))
        od = lax.dynamic_update_slice(od, od_s.reshape(ms, bw), (tp, 0))
        os_ = lax.dynamic_update_slice(os_, os_s.reshape(ms, bw), (tp, 0))

        merged = _branch_merge(d, sg_pool, od, os_, sg_att, gates, pool_w[l].astype(_BF16), pool_scale[l],
                               w_branch[l].astype(_BF16), tmb)
        last = l == depth - 1
        w_next = norm_f_w if last else norm_w[l + 1]
        res = _out_proj(merged, w_out[l].astype(_BF16), x, w_next, tp, tmb, last)
        if last:
            y_p, y_s = res
        else:
            x, h = res

        pool_p.append(u[tp - POOL_BUF:tp].reshape(1, POOL_BUF, bw))
        k_s.append(ks32.reshape(bs, ts, n_heads, HEAD_DIM))
        v_s.append(vs32.reshape(bs, ts, n_heads, HEAD_DIM))
        pool_s.append(jnp.concatenate([state_pool[l], u_s], axis=1)[:, -POOL_BUF:])

    return (y_p.reshape(1, tp, dm), y_s.reshape(bs, ts, dm),
            kp_stack.reshape(depth, 1, tp, n_heads, HEAD_DIM), vp_stack.reshape(depth, 1, tp, n_heads, HEAD_DIM),
            jnp.stack(pool_p), jnp.stack(k_s), jnp.stack(v_s), jnp.stack(pool_s))
```

```python
import functools
import math

import jax
import jax.numpy as jnp
from jax import lax
from jax.experimental import pallas as pl
from jax.experimental.pallas import tpu as pltpu

POOL_WINDOWS = (2, 4, 8, 16)
POOL_BUF = max(POOL_WINDOWS) - 1
POOL_HALO = 16
HEAD_DIM = 128
DK_DIFF = HEAD_DIM // 2
ROPE_THETA = 10000.0
EPS = 1e-6
SUBLN_EPS = 1e-5
NEG_BIG = -1e30
SB_CUTOFF = -104.0
LOG2E = math.log2(math.e)
PAGES_PER_STEP = 16
DIFF_Q_TILE = 1024
SB_Q_TILE = 256
SB_HEADS_PER_STEP = 4
SB_HEAD_PAGES = 2
SB_TAIL_PAGES_PER_STEP = 8
VMEM_LIMIT_BYTES = 56 * 1024 * 1024

_BF16 = jnp.bfloat16
_F32 = jnp.float32


def _lambda_init(layer):
    return 0.8 - 0.6 * math.exp(-0.3 * layer)


def _params(*sem):
    return pltpu.CompilerParams(dimension_semantics=sem, vmem_limit_bytes=VMEM_LIMIT_BYTES)


def _pow2_tile(cap, *sizes):
    t = cap
    while any(s % t for s in sizes):
        t //= 2
    return t


def _prompt_rows(i, ntp):
    return jnp.minimum(i, ntp - 1)


def _sample_rows(i, ntp):
    return jnp.maximum(i - ntp, 0)


def _rms_scale(x, w, eps):
    ms = jnp.mean(x * x, axis=-1, keepdims=True)
    return x * lax.rsqrt(ms + eps) * w


def _rmsnorm_kernel(xp_ref, xs_ref, w_ref, o_ref, *, ntp):
    x = jnp.where(pl.program_id(0) < ntp, xp_ref[...], xs_ref[...])
    o_ref[...] = _rms_scale(x, w_ref[...], EPS).astype(o_ref.dtype)


def _rmsnorm(xp, xs, w, out_dtype, tm):
    (tp, d), ms = xp.shape, xs.shape[0]
    ntp = tp // tm
    return pl.pallas_call(
        functools.partial(_rmsnorm_kernel, ntp=ntp),
        grid=((tp + ms) // tm,),
        in_specs=[pl.BlockSpec((tm, d), lambda i: (_prompt_rows(i, ntp), 0)),
                  pl.BlockSpec((tm, d), lambda i: (_sample_rows(i, ntp), 0)),
                  pl.BlockSpec((1, d), lambda i: (0, 0))],
        out_specs=pl.BlockSpec((tm, d), lambda i: (i, 0)),
        out_shape=jax.ShapeDtypeStruct((tp + ms, d), out_dtype),
        compiler_params=_params("arbitrary"),
        name="rmsnorm",
    )(xp, xs, w.reshape(1, d))


def _rope_heads(z, cos, sin_signed):
    tm = z.shape[0]
    lane = lax.broadcasted_iota(jnp.int32, (tm, HEAD_DIM), 1)
    first_half = (lane % DK_DIFF) < (DK_DIFF // 2)
    outs = []
    for h in range(z.shape[1] // HEAD_DIM):
        zh = z[:, h * HEAD_DIM:(h + 1) * HEAD_DIM]
        partner = jnp.where(first_half, pltpu.roll(zh, HEAD_DIM - DK_DIFF // 2, 1), pltpu.roll(zh, DK_DIFF // 2, 1))
        outs.append(zh * cos + partner * sin_signed)
    return jnp.concatenate(outs, axis=1)


def _proj_kernel(*refs, epilogue, scale, ntp, aliased):
    h_ref, w_ref, w16_scr = refs[0], refs[1], refs[-1]
    refs = refs[2:-1]
    j, i = pl.program_id(0), pl.program_id(1)

    @pl.when(i == 0)
    def _():
        w16_scr[...] = w_ref[...].astype(_BF16)

    z = jnp.dot(h_ref[...], w16_scr[...], preferred_element_type=_F32)

    def store_kv(val, op_ref, os_ref, o16_ref):
        o16_ref[...] = val.astype(_BF16)

        @pl.when(i < ntp)
        def _():
            op_ref[...] = val

        @pl.when(i >= ntp)
        def _():
            os_ref[...] = val

    if epilogue == "plain":
        refs[0][...] = z
    elif epilogue == "silu":
        refs[0][...] = z * jax.nn.sigmoid(z)
    elif epilogue == "scale_bf16":
        refs[0][...] = (z * scale).astype(_BF16)
    elif epilogue == "rope_scale_bf16":
        cos_ref, sin_ref, o_ref = refs
        o_ref[...] = (_rope_heads(z, cos_ref[...], sin_ref[...]) * scale).astype(_BF16)
    elif epilogue == "key":
        cos_ref, sin_ref = refs[:2]
        outs = refs[3:] if aliased else refs[2:]

        @pl.when(j == 0)
        def _():
            store_kv(_rope_heads(z, cos_ref[...], sin_ref[...]), *outs)

        @pl.when(j != 0)
        def _():
            store_kv(z, *outs)
    elif epilogue == "value":
        store_kv(z, *(refs[1:] if aliased else refs))
    elif epilogue == "gate":
        b_ref, o_ref = refs
        o_ref[...] = jax.nn.sigmoid(z + b_ref[...])
    else:
        raise ValueError(epilogue)


def _proj(h, w_all, layer, col_off, n_cols, epilogue, *, tm, tn, tp, extras=(), scale=1.0, stack=None):
    m, d = h.shape
    depth = w_all.shape[0]
    assert col_off % tn == 0 and n_cols % tn == 0 and tp % tm == 0
    off_blk = col_off // tn
    ntp = tp // tm
    grid = (n_cols // tn, m // tm)
    in_specs = [pl.BlockSpec((tm, d), lambda j, i: (i, 0)),
                pl.BlockSpec((None, d, tn), lambda j, i: (layer, 0, off_blk + j))]
    row_tab = pl.BlockSpec((tm, HEAD_DIM), lambda j, i: (i, 0))
    out_tile = pl.BlockSpec((tm, tn), lambda j, i: (i, j))
    f32_out = jax.ShapeDtypeStruct((m, n_cols), _F32)
    b16_out = jax.ShapeDtypeStruct((m, n_cols), _BF16)
    kv_specs = [pl.BlockSpec((None, tm, tn), lambda j, i: (layer, _prompt_rows(i, ntp), j)),
                pl.BlockSpec((tm, tn), lambda j, i: (_sample_rows(i, ntp), j)), out_tile]
    kv_shapes = [jax.ShapeDtypeStruct((depth, tp, n_cols), _F32), jax.ShapeDtypeStruct((m - tp, n_cols), _F32), b16_out]
    aliases = {}
    if epilogue in ("plain", "silu"):
        out_specs, out_shape = out_tile, f32_out
    elif epilogue == "scale_bf16":
        out_specs, out_shape = out_tile, b16_out
    elif epilogue == "rope_scale_bf16":
        in_specs += [row_tab, row_tab]
        out_specs, out_shape = out_tile, b16_out
    elif epilogue in ("key", "value"):
        if epilogue == "key":
            assert n_cols == 2 * tn
            in_specs += [row_tab, row_tab]
        out_specs, out_shape = kv_specs, kv_shapes
        if stack is not None:
            in_specs += [pl.BlockSpec(memory_space=pl.ANY)]
            extras = tuple(extras) + (stack,)
            aliases = {len(in_specs) - 1: 0}
    elif epilogue == "gate":
        in_specs += [pl.BlockSpec((1, tn), lambda j, i: (0, j))]
        out_specs, out_shape = out_tile, f32_out
    else:
        raise ValueError(epilogue)
    return pl.pallas_call(
        functools.partial(_proj_kernel, epilogue=epilogue, scale=scale, ntp=ntp, aliased=bool(aliases)),
        grid=grid, in_specs=in_specs, out_specs=out_specs, out_shape=out_shape,
        scratch_shapes=[pltpu.VMEM((d, tn), _BF16)],
        input_output_aliases=aliases,
        compiler_params=_params("arbitrary", "arbitrary"),
        name="proj_" + epilogue,
    )(h, w_all, *extras)


def _pool_prompt_kernel(u_ref, prev_ref, d_ref, *, tm, pg):
    i = pl.program_id(0)
    u = u_ref[...]
    prev = jnp.where(i > 0, prev_ref[...], 0.0)
    ext = jnp.concatenate([prev, u], axis=0)
    t = i * tm + lax.broadcasted_iota(jnp.int32, (tm, 1), 0)
    for g, w in enumerate(POOL_WINDOWS):
        s = ext[:, g * pg:(g + 1) * pg]
        sh = 1
        while sh < w:
            s = s + pltpu.roll(s, sh, 0)
            sh *= 2
        cnt = jnp.minimum(w, t + 1).astype(_F32)
        d_ref[:, g * pg:(g + 1) * pg] = s[POOL_HALO:] / cnt - u[:, g * pg:(g + 1) * pg]


def _pool_prompt(u, tp, tm):
    m, p = u.shape
    pg = p // len(POOL_WINDOWS)
    hb = tm // POOL_HALO
    return pl.pallas_call(
        functools.partial(_pool_prompt_kernel, tm=tm, pg=pg),
        grid=(tp // tm,),
        in_specs=[pl.BlockSpec((tm, p), lambda i: (i, 0)),
                  pl.BlockSpec((POOL_HALO, p), lambda i: (jnp.maximum(i * hb - 1, 0), 0))],
        out_specs=pl.BlockSpec((tm, p), lambda i: (i, 0)),
        out_shape=jax.ShapeDtypeStruct((m, p), _F32),
        compiler_params=_params("parallel"),
        name="pool_prompt",
    )(u, u)


def _pool_sample_kernel(st_ref, un_ref, d_ref, *, pg):
    nb, ts = st_ref.shape[0], un_ref.shape[0]
    for g, w in enumerate(POOL_WINDOWS):
        cols = slice(g * pg, (g + 1) * pg)
        rows = [st_ref[r, :, cols] for r in range(nb)] + [un_ref[t, :, cols] for t in range(ts)]
        for t in range(ts):
            r = nb + t
            lo = max(r + 1 - w, 0)
            s = rows[lo]
            for rr in range(lo + 1, r + 1):
                s = s + rows[rr]
            d_ref[t, :, cols] = s / float(r + 1 - lo) - rows[r]


def _pool_sample(state_t, u_new_t):
    nb, bs, p = state_t.shape
    ts = u_new_t.shape[0]
    pg = p // len(POOL_WINDOWS)
    return pl.pallas_call(
        functools.partial(_pool_sample_kernel, pg=pg),
        grid=(1,),
        in_specs=[pl.BlockSpec((nb, bs, p), lambda i: (0, 0, 0)), pl.BlockSpec((ts, bs, p), lambda i: (0, 0, 0))],
        out_specs=pl.BlockSpec((ts, bs, p), lambda i: (0, 0, 0)),
        out_shape=jax.ShapeDtypeStruct((ts, bs, p), _F32),
        compiler_params=_params("arbitrary"),
        name="pool_sample",
    )(state_t, u_new_t)


def _lambda_full(lam_ref, lam0):
    dl = lam_ref[...]
    a = jnp.sum(dl[0:1] * dl[1:2], axis=-1, keepdims=True)
    b = jnp.sum(dl[2:3] * dl[3:4], axis=-1, keepdims=True)
    return jnp.exp(a) - jnp.exp(b) + lam0


def _dot_nt(a, b):
    return lax.dot_general(a, b, (((1,), (1,)), ((), ())), preferred_element_type=_F32)


def _softmax_init(m_scr, l_scr, acc_scr):
    m_scr[...] = jnp.full(m_scr.shape, NEG_BIG, _F32)
    l_scr[...] = jnp.zeros(l_scr.shape, _F32)
    acc_scr[...] = jnp.zeros(acc_scr.shape, _F32)


def _softmax_update(s_parts, v_parts, m_scr, l_scr, acc_scr, valid=None):
    part_chunks = []
    for s in s_parts:
        chunks = [s[:, c * HEAD_DIM:(c + 1) * HEAD_DIM] for c in range(s.shape[1] // HEAD_DIM)]
        if valid is not None:
            chunks = [jnp.where(valid, c, NEG_BIG) for c in chunks]
        part_chunks.append(chunks)
    flat = [c for chunks in part_chunks for c in chunks]
    mx = flat[0]
    for c in flat[1:]:
        mx = jnp.maximum(mx, c)
    m_old = m_scr[...]
    m_new = jnp.maximum(m_old, jnp.max(mx, axis=-1, keepdims=True))
    alpha = jnp.exp2(m_old - m_new)
    acc = alpha * acc_scr[...]
    lsum = alpha * l_scr[...]
    for chunks, v16 in zip(part_chunks, v_parts):
        ps = [jnp.exp2(c - m_new) for c in chunks]
        for p in ps:
            lsum = lsum + p
        p16 = jnp.concatenate([p.astype(_BF16) for p in ps], axis=1)
        acc = acc + jnp.dot(p16, v16, preferred_element_type=_F32)
    l_scr[...] = lsum
    acc_scr[...] = acc
    m_scr[...] = m_new


def _softmax_result(l_scr, acc_scr):
    return acc_scr[...] / jnp.sum(l_scr[...], axis=-1, keepdims=True)


def _diff_prompt_kernel(lam_ref, subw_ref, q_ref, k_ref, v_ref, o_ref, m_scr, l_scr, acc_scr, *, lam0, tq):
    i = pl.program_id(1)
    q = q_ref[...]
    lane = lax.broadcasted_iota(jnp.int32, q.shape, 1)
    zero = jnp.zeros_like(q)
    qq = jnp.concatenate([jnp.where(lane < DK_DIFF, q, zero), jnp.where(lane >= DK_DIFF, q, zero)], axis=0)
    _softmax_init(m_scr, l_scr, acc_scr)

    def block(j, masked):
        start = pl.multiple_of(j * tq, tq)
        s = _dot_nt(qq, k_ref[pl.ds(start, tq), :])
        if masked:
            row = lax.broadcasted_iota(jnp.int32, s.shape, 0) % tq
            col = lax.broadcasted_iota(jnp.int32, s.shape, 1)
            s = jnp.where(col <= row, s, NEG_BIG)
        _softmax_update([s], [v_ref[pl.ds(start, tq), :]], m_scr, l_scr, acc_scr)

    def body(j, carry):
        block(j, False)
        return carry

    lax.fori_loop(0, i, body, 0)
    block(i, True)

    o = _softmax_result(l_scr, acc_scr)
    lam = _lambda_full(lam_ref, lam0)
    od = o[:tq] - lam * o[tq:]
    ms = jnp.mean(od * od, axis=-1, keepdims=True)
    o_ref[...] = od * lax.rsqrt(ms + SUBLN_EPS) * subw_ref[...] * (1.0 - lam0)


def _diff_prompt(lam_p, subw, q_d, k16, v16, tp, n_heads, m_rows, lam0, tq):
    return pl.pallas_call(
        functools.partial(_diff_prompt_kernel, lam0=lam0, tq=tq),
        grid=(n_heads, tp // tq),
        in_specs=[pl.BlockSpec(lam_p.shape, lambda h, i: (0, 0)),
                  pl.BlockSpec((1, HEAD_DIM), lambda h, i: (0, 0)),
                  pl.BlockSpec((tq, HEAD_DIM), lambda h, i: (i, h)),
                  pl.BlockSpec((tp, HEAD_DIM), lambda h, i: (0, h)),
                  pl.BlockSpec((tp, HEAD_DIM), lambda h, i: (0, h))],
        out_specs=pl.BlockSpec((tq, HEAD_DIM), lambda h, i: (i, h)),
        out_shape=jax.ShapeDtypeStruct((m_rows, n_heads * HEAD_DIM), _F32),
        scratch_shapes=[pltpu.VMEM((2 * tq, HEAD_DIM), _F32)] * 3,
        compiler_params=_params("parallel", "arbitrary"),
        name="diff_prompt",
    )(lam_p, subw, q_d, k16, v16)


def _log_sigmoid_pair(s):
    t = jnp.log1p(jnp.exp(-jnp.abs(s)))
    return jnp.minimum(s, 0.0) - t, -jnp.maximum(s, 0.0) - t


def _suffix_sums(lr, tri):
    hi = lr.astype(_BF16)
    lo = (lr - hi.astype(_F32)).astype(_BF16)
    return (jnp.dot(hi, tri, preferred_element_type=_F32) + jnp.dot(lo, tri, preferred_element_type=_F32))


def _strict_lower_ones(n):
    r = lax.broadcasted_iota(jnp.int32, (n, n), 0)
    c = lax.broadcasted_iota(jnp.int32, (n, n), 1)
    return (r > c).astype(_BF16)


def _sb_prompt_kernel(q_ref, k_ref, v_ref, o_ref, c_scr, acc_scr, *, tq, nh):
    i = pl.program_id(1)
    tri = _strict_lower_ones(tq)
    c_scr[...] = jnp.zeros(c_scr.shape, _F32)
    acc_scr[...] = jnp.zeros(acc_scr.shape, _F32)

    def block(j, masked):
        start = pl.multiple_of(j * tq, tq)
        for hh in range(nh):
            cols = slice(hh * HEAD_DIM, (hh + 1) * HEAD_DIM)
            s = _dot_nt(q_ref[:, cols], k_ref[pl.ds(start, tq), cols])
            lb, lr = _log_sigmoid_pair(s)
            if masked:
                row = lax.broadcasted_iota(jnp.int32, s.shape, 0)
                col = lax.broadcasted_iota(jnp.int32, s.shape, 1)
                valid = col < row
                lr = jnp.where(valid, lr, 0.0)
            c = c_scr[hh]
            a = jnp.exp(lb + _suffix_sums(lr, tri) + c)
            if masked:
                a = jnp.where(valid, a, 0.0)
            acc_scr[hh] += jnp.dot(a.astype(_BF16), v_ref[pl.ds(start, tq), cols], preferred_element_type=_F32)
            c_scr[hh] = c + jnp.sum(lr, axis=-1, keepdims=True)

    block(i, True)

    def cond(j):
        return jnp.logical_and(j >= 0, jnp.max(c_scr[...]) > SB_CUTOFF)

    def body(j):
        block(j, False)
        return j - 1

    lax.while_loop(cond, body, i - 1)
    for hh in range(nh):
        o_ref[:, hh * HEAD_DIM:(hh + 1) * HEAD_DIM] = acc_scr[hh]


def _sb_prompt(q_s, k16, v16, tp, n_heads, head_off, m_rows, tq):
    nh = SB_HEADS_PER_STEP
    assert n_heads % nh == 0 and head_off % nh == 0
    w = nh * HEAD_DIM
    return pl.pallas_call(
        functools.partial(_sb_prompt_kernel, tq=tq, nh=nh),
        grid=(n_heads // nh, tp // tq),
        in_specs=[pl.BlockSpec((tq, w), lambda h, i: (i, h)),
                  pl.BlockSpec((tp, w), lambda h, i: (0, head_off // nh + h)),
                  pl.BlockSpec((tp, w), lambda h, i: (0, head_off // nh + h))],
        out_specs=pl.BlockSpec((tq, w), lambda h, i: (i, h)),
        out_shape=jax.ShapeDtypeStruct((m_rows, n_heads * HEAD_DIM), _F32),
        scratch_shapes=[pltpu.VMEM((nh, tq, 1), _F32), pltpu.VMEM((nh, tq, HEAD_DIM), _F32)],
        compiler_params=_params("parallel", "arbitrary"),
        name="sb_prompt",
    )(q_s, k16, v16)


def _own_lanes(rows, rows_per_head, hg):
    lane = lax.broadcasted_iota(jnp.int32, (rows, HEAD_DIM), 1)
    row = lax.broadcasted_iota(jnp.int32, (rows, HEAD_DIM), 0)
    return lane, row, (lane % hg) == (row // rows_per_head)


def _group_rows(ref, hg):
    x = ref[...]
    return x.reshape(x.shape[0] * hg, HEAD_DIM).astype(_BF16)


def _sample_diff_kernel(pt_ref, lam_ref, subw_ref, qd_ref, kn_ref, vn_ref, *rest, lam0, ts, hg, gp):
    del pt_ref
    k_refs, v_refs = rest[:gp], rest[gp:2 * gp]
    od_ref, m_scr, l_scr, acc_scr = rest[2 * gp:]
    p = pl.program_id(1)
    qd = qd_ref[...]
    rd = qd.shape[0]
    lane, row, own = _own_lanes(rd, 2 * ts, hg)

    def block(k_refs, v_refs, valid):
        s_parts = [_dot_nt(qd, _group_rows(k_ref, hg)) for k_ref in k_refs]
        _softmax_update(s_parts, [_group_rows(v_ref, hg) for v_ref in v_refs], m_scr, l_scr, acc_scr, valid=valid)

    @pl.when(p == 0)
    def _():
        _softmax_init(m_scr, l_scr, acc_scr)
        block([kn_ref], [vn_ref], jnp.logical_and(own, jnp.logical_and(lane // hg < ts, lane // hg <= row % ts)))

    block(k_refs, v_refs, own)

    @pl.when(p == pl.num_programs(1) - 1)
    def _():
        o = _softmax_result(l_scr, acc_scr)
        lam = _lambda_full(lam_ref, lam0)
        od = o - lam * pltpu.roll(o, rd - ts, 0)
        ms = jnp.mean(od * od, axis=-1, keepdims=True)
        od_ref[...] = od * lax.rsqrt(ms + SUBLN_EPS) * subw_ref[...] * (1.0 - lam0)


def _sb_block(qs, k_ref, v_ref, valid, tmat_ref, run_scr, acc_scr, hg):
    rs = qs.shape[0]
    s = _dot_nt(qs, _group_rows(k_ref, hg))
    nch = s.shape[1] // HEAD_DIM
    lbs, lrs = [], []
    for c in range(nch):
        lb, lr = _log_sigmoid_pair(s[:, c * HEAD_DIM:(c + 1) * HEAD_DIM])
        lbs.append(lb)
        lrs.append(jnp.where(valid, lr, 0.0))
    lr_all = jnp.concatenate(lrs, axis=0)
    hi = lr_all.astype(_BF16)
    lo = (lr_all - hi.astype(_F32)).astype(_BF16)
    tm = tmat_ref[...]
    r = jnp.dot(hi, tm, preferred_element_type=_F32) + jnp.dot(lo, tm, preferred_element_type=_F32)
    run = run_scr[...]
    a_parts = [None] * nch
    for c in reversed(range(nch)):
        rc = r[c * rs:(c + 1) * rs]
        a = jnp.exp(lbs[c] + rc[:, :HEAD_DIM] + run)
        a_parts[c] = jnp.where(valid, a, 0.0).astype(_BF16)
        run = run + rc[:, HEAD_DIM:]
    run_scr[...] = run
    acc_scr[...] += jnp.dot(jnp.concatenate(a_parts, axis=1), _group_rows(v_ref, hg), preferred_element_type=_F32)


def _sb_pages(qs, k_refs, v_refs, own, tmat_ref, run_scr, acc_scr, hg):
    def go(g):
        @pl.when(jnp.max(jnp.where(own, run_scr[...], NEG_BIG)) > SB_CUTOFF)
        def _():
            _sb_block(qs, k_refs[g], v_refs[g], own, tmat_ref, run_scr, acc_scr, hg)
            if g + 1 < len(k_refs):
                go(g + 1)

    go(0)


def _sample_sb_head_kernel(pt_ref, tmat_ref, qs_ref, kn_ref, vn_ref, *rest, ts, hg, gp):
    del pt_ref
    k_refs, v_refs = rest[:gp], rest[gp:2 * gp]
    acc_ref, run_ref, run_scr, acc_scr = rest[2 * gp:]
    qs = qs_ref[...]
    lane, row, own = _own_lanes(qs.shape[0], ts, hg)
    run_scr[...] = jnp.zeros(run_scr.shape, _F32)
    acc_scr[...] = jnp.zeros(acc_scr.shape, _F32)
    _sb_block(qs, kn_ref, vn_ref, jnp.logical_and(own, jnp.logical_and(lane // hg < ts, lane // hg < row % ts)),
              tmat_ref, run_scr, acc_scr, hg)
    _sb_pages(qs, k_refs, v_refs, own, tmat_ref, run_scr, acc_scr, hg)
    acc_ref[...] = acc_scr[...]
    run_ref[...] = run_scr[...]


def _sample_sb_tail_kernel(pt_ref, live_ref, tmat_ref, qs_ref, acc_in_ref, run_in_ref, *rest, ts, hg, gp):
    del pt_ref
    k_refs, v_refs = rest[:gp], rest[gp:2 * gp]
    acc_ref, run_scr, acc_scr = rest[2 * gp:]
    b, p = pl.program_id(0), pl.program_id(1)
    qs = qs_ref[...]
    _, _, own = _own_lanes(qs.shape[0], ts, hg)

    @pl.when(p == 0)
    def _():
        run_scr[...] = run_in_ref[...]
        acc_scr[...] = acc_in_ref[...]

    @pl.when(live_ref[b] != 0)
    def _():
        _sb_pages(qs, k_refs, v_refs, own, tmat_ref, run_scr, acc_scr, hg)

    @pl.when(p == pl.num_programs(1) - 1)
    def _():
        acc_ref[...] = acc_scr[...]


def _suffix_matrix(hg):
    idx = jnp.arange(HEAD_DIM)
    tok, head = idx // hg, idx % hg
    same = head[:, None] == head[None, :]
    later = jnp.logical_and(same, tok[:, None] > tok[None, :])
    return jnp.concatenate([later, same], axis=1).astype(_BF16)


def _largest_divisor(n, cap):
    return max(d for d in range(1, cap + 1) if n % d == 0)


def _sample_attn(page_table, lam_p, subw, qd_rows, qs_rows, kn, vn, cache_k, cache_v, layer, lam0, ts):
    bs, n_pages = page_table.shape
    page, n_heads = cache_k.shape[2], cache_k.shape[3]
    hg = n_heads // 2
    rd, rs = qd_rows.shape[1], qs_rows.shape[1]
    new_tok = kn.shape[1]
    assert hg == 8 and new_tok * hg == HEAD_DIM and (page * hg) % HEAD_DIM == 0
    pt_flat = page_table.reshape(-1)
    tmat = _suffix_matrix(hg)
    half = (None, None, page, hg, HEAD_DIM)
    state = jax.ShapeDtypeStruct((bs, rs, HEAD_DIM), _F32)

    def per_b(shape):
        return pl.BlockSpec((None,) + shape, lambda b, *_: (b,) + (0,) * len(shape))

    def const(shape):
        return pl.BlockSpec(shape, lambda *_: (0,) * len(shape))

    def new_keys(group):
        return pl.BlockSpec((None, new_tok, hg, HEAD_DIM), lambda b, *_: (b, 0, group, 0))

    gp = _pow2_tile(PAGES_PER_STEP, n_pages)

    def diff_page(g):
        return pl.BlockSpec(half, lambda b, p, pt: (layer, pt[b * n_pages + (n_pages - 1 - (p * gp + g))], 0, 0, 0))

    od = pl.pallas_call(
        functools.partial(_sample_diff_kernel, lam0=lam0, ts=ts, hg=hg, gp=gp),
        grid_spec=pltpu.PrefetchScalarGridSpec(
            num_scalar_prefetch=1, grid=(bs, n_pages // gp),
            in_specs=[const(lam_p.shape), const((1, HEAD_DIM)), per_b((rd, HEAD_DIM)), new_keys(0), new_keys(0)]
            + [diff_page(g) for g in range(gp)] + [diff_page(g) for g in range(gp)],
            out_specs=per_b((rd, HEAD_DIM)),
            scratch_shapes=[pltpu.VMEM((rd, HEAD_DIM), _F32)] * 3),
        out_shape=jax.ShapeDtypeStruct((bs, rd, HEAD_DIM), _F32),
        compiler_params=_params("parallel", "arbitrary"),
        name="sample_diff",
    )(pt_flat, lam_p, subw, qd_rows, kn, vn, *([cache_k] * gp), *([cache_v] * gp))

    g1 = min(SB_HEAD_PAGES, n_pages)

    def head_page(g):
        return pl.BlockSpec(half, lambda b, pt: (layer, pt[b * n_pages + (n_pages - 1 - g)], 0, 1, 0))

    acc, run = pl.pallas_call(
        functools.partial(_sample_sb_head_kernel, ts=ts, hg=hg, gp=g1),
        grid_spec=pltpu.PrefetchScalarGridSpec(
            num_scalar_prefetch=1, grid=(bs,),
            in_specs=[const(tmat.shape), per_b((rs, HEAD_DIM)), new_keys(1), new_keys(1)]
            + [head_page(g) for g in range(g1)] + [head_page(g) for g in range(g1)],
            out_specs=[per_b((rs, HEAD_DIM)), per_b((rs, HEAD_DIM))],
            scratch_shapes=[pltpu.VMEM((rs, HEAD_DIM), _F32)] * 2),
        out_shape=[state, state],
        compiler_params=_params("parallel"),
        name="sample_sb_head",
    )(pt_flat, tmat, qs_rows, kn, vn, *([cache_k] * g1), *([cache_v] * g1))

    n_rest = n_pages - g1
    if n_rest:
        lane = jnp.arange(HEAD_DIM)[None, :] % hg
        head = jnp.arange(rs)[:, None] // ts
        live = (jnp.max(jnp.where(lane == head, run, NEG_BIG), axis=(1, 2)) > SB_CUTOFF).astype(jnp.int32)
        gt = _largest_divisor(n_rest, SB_TAIL_PAGES_PER_STEP)

        def tail_page(g):
            def index(b, p, pt, lv):
                phys = pt[b * n_pages + (n_pages - 1 - (g1 + p * gt + g))]
                return (layer, jnp.where(lv[b] != 0, phys, 0), 0, 1, 0)
            return pl.BlockSpec(half, index)

        def tail(acc_in):
            return pl.pallas_call(
                functools.partial(_sample_sb_tail_kernel, ts=ts, hg=hg, gp=gt),
                grid_spec=pltpu.PrefetchScalarGridSpec(
                    num_scalar_prefetch=2, grid=(bs, n_rest // gt),
                    in_specs=[const(tmat.shape), per_b((rs, HEAD_DIM)), per_b((rs, HEAD_DIM)), per_b((rs, HEAD_DIM))]
                    + [tail_page(g) for g in range(gt)] + [tail_page(g) for g in range(gt)],
                    out_specs=per_b((rs, HEAD_DIM)),
                    scratch_shapes=[pltpu.VMEM((rs, HEAD_DIM), _F32)] * 2),
                out_shape=state,
                compiler_params=_params("parallel", "arbitrary"),
                name="sample_sb_tail",
            )(pt_flat, live, tmat, qs_rows, acc_in, run, *([cache_k] * gt), *([cache_v] * gt))

        acc = lax.cond(jnp.any(live != 0), tail, lambda acc_in: acc_in, acc)
    return od, acc


def _sample_query_rows(q_d, q_s, bs, ts, n_diff, n_sb):
    qd = q_d.reshape(bs, ts, n_diff, 2, DK_DIFF)
    z = jnp.zeros_like(qd[..., 0, :])
    comps = jnp.stack([jnp.concatenate([qd[..., 0, :], z], axis=-1),
                       jnp.concatenate([z, qd[..., 1, :]], axis=-1)], axis=3)
    qd_rows = jnp.transpose(comps, (0, 2, 3, 1, 4)).reshape(bs, n_diff * 2 * ts, HEAD_DIM)
    qs_rows = jnp.transpose(q_s.reshape(bs, ts, n_sb, HEAD_DIM), (0, 2, 1, 3)).reshape(bs, n_sb * ts, HEAD_DIM)
    return qd_rows, qs_rows


def _branch_kernel(d_ref, sgp_ref, od_ref, os_ref, sga_ref, gate_ref, pw_ref, ps_ref, wb_ref, o_ref, *, pg):
    p = d_ref.shape[1]
    dm = o_ref.shape[1]
    d = d_ref[...]
    ys = [jnp.dot(d[:, g * pg:(g + 1) * pg].astype(_BF16), pw_ref[g], preferred_element_type=_F32)
          for g in range(len(POOL_WINDOWS))]
    y_pool = jnp.concatenate(ys, axis=1) * ps_ref[...]
    b0 = (y_pool * sgp_ref[...]).astype(_BF16)
    b1 = (od_ref[...] * sga_ref[:, :p]).astype(_BF16)
    b2 = (os_ref[...] * sga_ref[:, p:]).astype(_BF16)
    merged = gate_ref[:, 0:dm] * jnp.dot(b0, wb_ref[0], preferred_element_type=_F32)
    merged += gate_ref[:, dm:2 * dm] * jnp.dot(b1, wb_ref[1], preferred_element_type=_F32)
    merged += gate_ref[:, 2 * dm:3 * dm] * jnp.dot(b2, wb_ref[2], preferred_element_type=_F32)
    o_ref[...] = merged.astype(_BF16)


def _branch_merge(d, sg_pool, od, os_, sg_att, gates, pool_w16, pool_scale, w_branch16, tm):
    m, p = d.shape
    dm = w_branch16.shape[2]
    pg = p // len(POOL_WINDOWS)
    row = lambda w: pl.BlockSpec((tm, w), lambda i: (i, 0))
    const = lambda shape: pl.BlockSpec(shape, lambda i: (0,) * len(shape), pipeline_mode=pl.Buffered(1))
    return pl.pallas_call(
        functools.partial(_branch_kernel, pg=pg),
        grid=(m // tm,),
        in_specs=[row(p), row(p), row(p), row(p), row(2 * p), row(3 * dm),
                  const(pool_w16.shape), const((1, p)), const(w_branch16.shape)],
        out_specs=row(dm),
        out_shape=jax.ShapeDtypeStruct((m, dm), _BF16),
        compiler_params=_params("parallel"),
        name="branch_merge",
    )(d, sg_pool, od, os_, sg_att, gates, pool_w16, pool_scale.reshape(1, p), w_branch16)


def _out_kernel(*refs, x_pair, final, ntp):
    i = pl.program_id(0)
    mg_ref, wo_ref = refs[:2]
    if x_pair:
        x_in = jnp.where(i < ntp, refs[2][...], refs[3][...])
        refs = refs[4:]
    else:
        x_in = refs[2][...]
        refs = refs[3:]
    wn_ref = refs[0]
    x = x_in + jnp.dot(mg_ref[...], wo_ref[...], preferred_element_type=_F32)
    hn = _rms_scale(x, wn_ref[...], EPS)
    if final:
        yp_ref, ys_ref = refs[1:]

        @pl.when(i < ntp)
        def _():
            yp_ref[...] = hn

        @pl.when(i >= ntp)
        def _():
            ys_ref[...] = hn
    else:
        xo_ref, hn_ref = refs[1:]
        xo_ref[...] = x
        hn_ref[...] = hn.astype(hn_ref.dtype)


def _out_proj(merged, w_out16, x, w_next, tp, tm, final):
    m, dm = merged.shape
    ntp = tp // tm
    x_pair = isinstance(x, tuple)
    row = pl.BlockSpec((tm, dm), lambda i: (i, 0))
    p_rows = pl.BlockSpec((tm, dm), lambda i: (_prompt_rows(i, ntp), 0))
    s_rows = pl.BlockSpec((tm, dm), lambda i: (_sample_rows(i, ntp), 0))
    in_specs = [row, pl.BlockSpec((dm, dm), lambda i: (0, 0), pipeline_mode=pl.Buffered(1))]
    in_specs += [p_rows, s_rows] if x_pair else [row]
    in_specs += [pl.BlockSpec((1, dm), lambda i: (0, 0))]
    if final:
        out_specs = [p_rows, s_rows]
        out_shape = [jax.ShapeDtypeStruct((tp, dm), _F32), jax.ShapeDtypeStruct((m - tp, dm), _F32)]
    else:
        out_specs = [row, row]
        out_shape = [jax.ShapeDtypeStruct((m, dm), _F32), jax.ShapeDtypeStruct((m, dm), _BF16)]
    return pl.pallas_call(
        functools.partial(_out_kernel, x_pair=x_pair, final=final, ntp=ntp),
        grid=(m // tm,),
        in_specs=in_specs, out_specs=out_specs, out_shape=out_shape,
        compiler_params=_params("arbitrary"),
        name="out_proj",
    )(merged, w_out16, *(x if x_pair else (x,)), w_next.reshape(1, dm))


def _rope_tables(pos):
    half = DK_DIFF // 2
    inv = 1.0 / (ROPE_THETA ** (jnp.arange(half, dtype=_F32) * 2.0 / DK_DIFF))
    ang = pos.astype(_F32)[:, None] * inv[None, :]
    cos, sin = jnp.cos(ang), jnp.sin(ang)
    reps = HEAD_DIM // DK_DIFF
    return jnp.tile(jnp.concatenate([cos, cos], axis=1), (1, reps)), jnp.tile(jnp.concatenate([-sin, sin], axis=1), (1, reps))


def kernel(x_prompt, x_sample, cache_k, cache_v, state_pool, page_table, norm_w, w_in, gate_b, pool_w, pool_scale,
           diff_lambda, diff_subln_w, w_branch, w_out, norm_f_w):
    bp, tp, dm = x_prompt.shape
    bs, ts, _ = x_sample.shape
    assert bp == 1
    depth = w_in.shape[0]
    n_pages = page_table.shape[1]
    page, n_heads = cache_k.shape[2], cache_k.shape[3]
    assert cache_k.shape[4] == HEAD_DIM and state_pool.shape[2] == POOL_BUF
    ms = bs * ts
    m = tp + ms
    bw = dm // 2
    n_diff = bw // HEAD_DIM
    n_sb = bw // HEAD_DIM
    assert n_heads == n_diff + n_sb
    kvw = n_heads * HEAD_DIM
    offs = {}
    o = 0
    for name, size in (("u", bw), ("g_pool", bw), ("q_d", bw), ("q_s", bw), ("k", kvw), ("v", kvw),
                       ("g_att", 2 * bw), ("g_br", 3 * dm)):
        offs[name] = o
        o += size
    assert o == w_in.shape[2]

    tm = _pow2_tile(512, tp, ms)
    tn = _pow2_tile(1024, bw)
    tq = _pow2_tile(DIFF_Q_TILE, tp)
    tqs = _pow2_tile(SB_Q_TILE, tp)
    tmb = _pow2_tile(256, tp, ms)

    pos = jnp.concatenate([jnp.arange(tp, dtype=jnp.int32),
                           jnp.tile(n_pages * page + jnp.arange(ts, dtype=jnp.int32), bs)])
    cos_t, sin_t = _rope_tables(pos)

    x = (x_prompt.reshape(tp, dm), x_sample.reshape(ms, dm))
    h = _rmsnorm(*x, norm_w[0], _BF16, tm)

    kp_stack = vp_stack = y_p = y_s = None
    pool_p, k_s, v_s, pool_s = [], [], [], []
    for l in range(depth):
        lam0 = _lambda_init(l)
        proj = functools.partial(_proj, h, w_in, l, tm=tm, tn=tn, tp=tp)
        u = proj(offs["u"], bw, "plain")
        sg_pool = proj(offs["g_pool"], bw, "silu")
        q_d = proj(offs["q_d"], bw, "rope_scale_bf16", extras=(cos_t, sin_t), scale=DK_DIFF ** -0.5 * LOG2E)
        q_s = proj(offs["q_s"], bw, "scale_bf16", scale=HEAD_DIM ** -0.5)
        kp_stack, ks32, k16 = proj(offs["k"], kvw, "key", extras=(cos_t, sin_t), stack=kp_stack)
        vp_stack, vs32, v16 = proj(offs["v"], kvw, "value", stack=vp_stack)
        sg_att = proj(offs["g_att"], 2 * bw, "silu")
        gates = proj(offs["g_br"], 3 * dm, "gate", extras=(gate_b[l].reshape(1, 3 * dm),))

        d = _pool_prompt(u, tp, tm)
        u_s = u[tp:].reshape(bs, ts, bw)
        d_s = _pool_sample(jnp.transpose(state_pool[l], (1, 0, 2)), jnp.transpose(u_s, (1, 0, 2)))
        d = lax.dynamic_update_slice(d, jnp.transpose(d_s, (1, 0, 2)).reshape(ms, bw), (tp, 0))

        lam_p = diff_lambda[l]
        subw = diff_subln_w[l].reshape(1, HEAD_DIM)
        od = _diff_prompt(lam_p, subw, q_d, k16, v16, tp, n_diff, m, lam0, tq)
        os_ = _sb_prompt(q_s, k16, v16, tp, n_sb, n_diff, m, tqs)
        qd_rows, qs_rows = _sample_query_rows(q_d[tp:], q_s[tp:], bs, ts, n_diff, n_sb)
        new_tok = HEAD_DIM // (n_heads // 2)
        pad = ((0, 0), (0, new_tok - ts), (0, 0), (0, 0))
        kn = jnp.pad(ks32.reshape(bs, ts, n_heads, HEAD_DIM), pad)
        vn = jnp.pad(vs32.reshape(bs, ts, n_heads, HEAD_DIM), pad)
        od_s, os_s = _sample_attn(page_table, lam_p, subw, qd_rows, qs_rows, kn, vn, cache_k, cache_v, l, lam0, ts)
        od_s = jnp.transpose(od_s.reshape(bs, n_diff, 2 * ts, HEAD_DIM)[:, :, :ts], (0, 2, 1, 3))
        os_s = jnp.transpose(os_s.reshape(bs, n_sb, ts, HEAD_DIM), (0, 2, 1, 3))
        od = lax.dynamic_update_slice(od, od_s.reshape(ms, bw), (tp, 0))
        os_ = lax.dynamic_update_slice(os_, os_s.reshape(ms, bw), (tp, 0))

        merged = _branch_merge(d, sg_pool, od, os_, sg_att, gates, pool_w[l].astype(_BF16), pool_scale[l],
                               w_branch[l].astype(_BF16), tmb)
        last = l == depth - 1
        w_next = norm_f_w if last else norm_w[l + 1]
        res = _out_proj(merged, w_out[l].astype(_BF16), x, w_next, tp, tmb, last)
        if last:
            y_p, y_s = res
        else:
            x, h = res

        pool_p.append(u[tp - POOL_BUF:tp].reshape(1, POOL_BUF, bw))
        k_s.append(ks32.reshape(bs, ts, n_heads, HEAD_DIM))
        v_s.append(vs32.reshape(bs, ts, n_heads, HEAD_DIM))
        pool_s.append(jnp.concatenate([state_pool[l], u_s], axis=1)[:, -POOL_BUF:])

    return (y_p.reshape(1, tp, dm), y_s.reshape(bs, ts, dm),
            kp_stack.reshape(depth, 1, tp, n_heads, HEAD_DIM), vp_stack.reshape(depth, 1, tp, n_heads, HEAD_DIM),
            jnp.stack(pool_p), jnp.stack(k_s), jnp.stack(v_s), jnp.stack(pool_s))
```
